```python
import jax
import jax.numpy as jnp
from jax import lax
import numpy as np

D_MODEL = 1024
BATCH = 4
SEQ = 8192
DEPTH = 2

GRID_W = 64
CTX_LEN = 256
EPS = 1e-6

CONV_DIM = 512
CONV_WIDTH = 31
MLSTM_HEADS = 4
MLSTM_QK = 64
MLSTM_V = 128
MLSTM_CHUNK = 64
AB_SIZES = (CONV_DIM, CONV_DIM, MLSTM_HEADS * MLSTM_QK, MLSTM_HEADS * MLSTM_QK,
            MLSTM_HEADS * MLSTM_V, MLSTM_HEADS * MLSTM_V, 4 * MLSTM_HEADS)
AB_IN = sum(AB_SIZES)
AB_OUT = CONV_DIM + MLSTM_HEADS * MLSTM_V

MLA_HEADS = 16
MLA_Q_LORA = 512
MLA_KV_LORA = 256
MLA_NOPE = 64
MLA_ROPE = 32
MLA_V = 64
MLA_QK = MLA_NOPE + MLA_ROPE
MLA_IN = MLA_Q_LORA + MLA_KV_LORA + MLA_ROPE
ROPE_BASE = 10000.0
ATTN_BLOCK = 128

FFN_DIM = 2816
FFN_CONV_WIDTH = 3

kernel_name = 'hybrid_conformer_mlstm_mla_prefix_dit'


def _split_cols(a, sizes):
    return jnp.split(a, np.cumsum(sizes)[:-1].tolist(), axis=-1)


def _rms(x):
    xf = x.astype(jnp.float32)
    return (xf * lax.rsqrt(jnp.mean(xf * xf, axis=-1, keepdims=True) + EPS)).astype(x.dtype)


def _layer_norm(x, g, b):
    xf = x.astype(jnp.float32)
    mu = jnp.mean(xf, axis=-1, keepdims=True)
    var = jnp.mean(jnp.square(xf - mu), axis=-1, keepdims=True)
    return ((xf - mu) * lax.rsqrt(var + EPS)).astype(x.dtype) * g + b


def _modulate(x, shift, scale):
    return _rms(x) * (1 + scale) + shift


def _dwconv(x, w, b):
    k, ch = w.shape
    pad = (k - 1) // 2
    y = lax.conv_general_dilated(x, w[:, None, :].astype(x.dtype), (1,), [(pad, k - 1 - pad)],
                                 dimension_numbers=('NWC', 'WIO', 'NWC'), feature_group_count=ch)
    return y + b


def _zero_state(batch):
    return (jnp.zeros((batch, MLSTM_HEADS, MLSTM_V, MLSTM_QK), jnp.float32),
            jnp.zeros((batch, MLSTM_HEADS, MLSTM_QK), jnp.float32),
            jnp.zeros((batch, MLSTM_HEADS), jnp.float32))


def _mlstm_chunkwise(q, k, v, log_i, log_f, state):
    bsz, nh, t, _ = q.shape
    nc = t // MLSTM_CHUNK

    def chunks(a):
        return jnp.moveaxis(a.reshape(bsz, nh, nc, MLSTM_CHUNK, *a.shape[3:]), 2, 0)

    lower = jnp.tril(jnp.ones((MLSTM_CHUNK, MLSTM_CHUNK), bool))

    def step(carry, inp):
        c_mat, n_vec, m = carry
        qj, kj, vj, li, lf = inp
        b = jnp.cumsum(lf, axis=-1)
        log_d = jnp.where(lower, b[..., :, None] - b[..., None, :] + li[..., None, :], -jnp.inf)
        log_inter = b + m[..., None]
        m_row = jnp.maximum(log_inter, jnp.max(log_d, axis=-1))
        s = jnp.einsum('bhid,bhjd->bhij', qj, kj) * jnp.exp(log_d - m_row[..., None])
        inter = jnp.exp(log_inter - m_row)
        num = jnp.einsum('bhij,bhjv->bhiv', s, vj) + inter[..., None] * jnp.einsum('bhvd,bhid->bhiv', c_mat, qj)
        den = jnp.sum(s, axis=-1) + inter * jnp.einsum('bhd,bhid->bhi', n_vec, qj)
        h = num / jnp.maximum(jnp.abs(den), jnp.exp(-m_row))[..., None]
        b_last = b[..., -1]
        log_w = b_last[..., None] - b + li
        m_new = jnp.maximum(b_last + m, jnp.max(log_w, axis=-1))
        w = jnp.exp(log_w - m_new[..., None])
        decay = jnp.exp(b_last + m - m_new)
        c_new = decay[..., None, None] * c_mat + jnp.einsum('bhj,bhjv,bhjd->bhvd', w, vj, kj)
        n_new = decay[..., None] * n_vec + jnp.einsum('bhj,bhjd->bhd', w, kj)
        return (c_new, n_new, m_new), h

    state, h = lax.scan(step, state, (chunks(q), chunks(k), chunks(v), chunks(log_i), chunks(log_f)))
    return jnp.moveaxis(h, 0, 2).reshape(bsz, nh, t, -1), state


def _flip(t):
    return jnp.flip(t, axis=2)


def _ab_project(h, w_in, gate_bias):
    a, g, q, k, v, o, gates = _split_cols(h @ w_in, AB_SIZES)

    def heads(t, d):
        return t.reshape(*t.shape[:2], -1, d).transpose(0, 2, 1, 3).astype(jnp.float32)

    q = heads(q, MLSTM_QK)
    k = heads(k, MLSTM_QK) * (MLSTM_QK ** -0.5)
    v = heads(v, MLSTM_V)
    gates = (gates + gate_bias).astype(jnp.float32).transpose(0, 2, 1)
    ig_f, fg_f, ig_b, fg_b = jnp.split(gates, 4, axis=1)
    fwd = (ig_f, jax.nn.log_sigmoid(fg_f))
    bwd = (ig_b, jax.nn.log_sigmoid(fg_b))
    return a, g, q, k, v, o, fwd, bwd


def _conformer_mlstm_mixer(h_lat, h_ctx, w_in, gate_bias, conv_w, conv_b, ln_g, ln_b,
                           head_gain, w_out, with_ctx_out):
    lat = _ab_project(h_lat, w_in, gate_bias)
    ctx = _ab_project(h_ctx, w_in, gate_bias)
    zero = _zero_state(h_lat.shape[0])

    def scans(p, init_f, init_b):
        _, _, q, k, v, _, fwd, bwd = p
        h_f, st_f = _mlstm_chunkwise(q, k, v, fwd[0], fwd[1], init_f)
        h_b, st_b = _mlstm_chunkwise(_flip(q), _flip(k), _flip(v), _flip(bwd[0]), _flip(bwd[1]), init_b)
        return h_f + _flip(h_b), st_f, st_b

    h_ctx_seq, st_f, st_b = scans(ctx, zero, zero)
    h_lat_seq, _, _ = scans(lat, st_f, st_b)

    def merge(p, h_seq):
        a, g, _, _, _, o, _, _ = p
        u = _dwconv(a * jax.nn.sigmoid(g), conv_w, conv_b)
        u = jax.nn.silu(_layer_norm(u, ln_g, ln_b))
        hm = _rms(h_seq).transpose(0, 2, 1, 3)
        hm = hm.reshape(*hm.shape[:2], -1).astype(o.dtype) * head_gain * jax.nn.sigmoid(o)
        return jnp.concatenate([u, hm], axis=-1) @ w_out

    y_lat = merge(lat, h_lat_seq)
    y_ctx = merge(ctx, h_ctx_seq) if with_ctx_out else None
    return y_lat, y_ctx


def _axial_rope_tables(rows):
    row = jnp.repeat(jnp.arange(rows, dtype=jnp.float32), GRID_W)
    col = jnp.tile(jnp.arange(GRID_W, dtype=jnp.float32), rows)
    half = MLA_ROPE // 2
    inv_freq = ROPE_BASE ** (-jnp.arange(0, half, 2, dtype=jnp.float32) / half)
    ang = jnp.concatenate([row[:, None] * inv_freq, col[:, None] * inv_freq], axis=-1)
    return jnp.cos(ang), jnp.sin(ang)


def _rope_tail(t, cos, sin):
    nope, r = t[..., :MLA_NOPE], t[..., MLA_NOPE:]
    r = r.reshape(*r.shape[:-1], -1, 2)
    c = cos[None, :, None, :].astype(t.dtype)
    s = sin[None, :, None, :].astype(t.dtype)
    r1, r2 = r[..., 0], r[..., 1]
    rot = jnp.stack([r1 * c - r2 * s, r1 * s + r2 * c], axis=-1).reshape(*nope.shape[:-1], MLA_ROPE)
    return jnp.concatenate([nope, rot], axis=-1)


def _block_attention(q, k, v):
    bsz, tq, nh, dk = q.shape
    nb = tq // ATTN_BLOCK
    qb = jnp.moveaxis(q.reshape(bsz, nb, ATTN_BLOCK, nh, dk), 1, 0)
    scale = dk ** -0.5

    def one_block(qi):
        s = jnp.einsum('bqhd,bkhd->bhqk', qi, k).astype(jnp.float32) * scale
        p = jax.nn.softmax(s, axis=-1).astype(v.dtype)
        return jnp.einsum('bhqk,bkhd->bqhd', p, v)

    out = lax.map(one_block, qb)
    return jnp.moveaxis(out, 0, 1).reshape(bsz, tq, nh * v.shape[-1])


def _mla_mixer(h_lat, h_ctx, cos, sin, w_in, q_norm, kv_norm, w_uq, w_ukv, q_gain, k_gain,
               w_out, with_ctx_out):
    def queries(cq, rotate):
        q = (_rms(cq) * q_norm) @ w_uq
        q = _rms(q.reshape(*q.shape[:2], MLA_HEADS, MLA_QK)) * q_gain
        return _rope_tail(q, cos, sin) if rotate else q

    def keys_values(ckv, k_rope, rotate):
        ukv = ((_rms(ckv) * kv_norm) @ w_ukv).reshape(*ckv.shape[:2], MLA_HEADS, MLA_NOPE + MLA_V)
        k_nope, v = ukv[..., :MLA_NOPE], ukv[..., MLA_NOPE:]
        k_rope = jnp.broadcast_to(k_rope[:, :, None, :], (*k_nope.shape[:3], MLA_ROPE))
        k = _rms(jnp.concatenate([k_nope, k_rope], axis=-1)) * k_gain
        return (_rope_tail(k, cos, sin) if rotate else k), v

    cq_l, ckv_l, kr_l = _split_cols(h_lat @ w_in, (MLA_Q_LORA, MLA_KV_LORA, MLA_ROPE))
    ckv_c, kr_c = _split_cols(h_ctx @ w_in[:, MLA_Q_LORA:], (MLA_KV_LORA, MLA_ROPE))
    k_l, v_l = keys_values(ckv_l, kr_l, True)
    k_c, v_c = keys_values(ckv_c, kr_c, False)
    k_all = jnp.concatenate([k_c, k_l], axis=1)
    v_all = jnp.concatenate([v_c, v_l], axis=1)
    y_lat = _block_attention(queries(cq_l, True), k_all, v_all) @ w_out
    y_ctx = None
    if with_ctx_out:
        q_c = queries(h_ctx @ w_in[:, :MLA_Q_LORA], False)
        y_ctx = _block_attention(q_c, k_c, v_c) @ w_out
    return y_lat, y_ctx


def _conv_ffn(h, w_in, conv_w, conv_b, w_out):
    g, val = jnp.split(h @ w_in, 2, axis=-1)
    return (jax.nn.gelu(_dwconv(g, conv_w, conv_b), approximate=True) * val) @ w_out


def setup_inputs(seed: int = 0) -> dict:
    key = jax.random.key(seed)
    ks = iter(jax.random.split(key, 32))
    n_even = (DEPTH + 1) // 2
    n_odd = DEPTH // 2
    f32 = jnp.float32

    def nrm(shape, scale):
        return jax.random.normal(next(ks), shape, f32) * scale

    def gain(shape):
        return 1.0 + nrm(shape, 0.1)

    inp = {}
    inp['x'] = nrm((BATCH, SEQ, D_MODEL), 1.0)
    inp['c'] = nrm((BATCH, D_MODEL), 1.0)
    inp['ctx'] = nrm((BATCH, CTX_LEN, D_MODEL), 1.0)
    inp['c_ctx'] = nrm((D_MODEL,), 1.0)
    inp['ada_w'] = nrm((DEPTH, D_MODEL, 6 * D_MODEL), 0.5 * D_MODEL ** -0.5)
    inp['ada_b'] = nrm((DEPTH, 6 * D_MODEL), 0.02)
    inp['ab_w_in'] = nrm((n_even, D_MODEL, AB_IN), D_MODEL ** -0.5)
    ig = nrm((n_even, 2, 1, MLSTM_HEADS), 0.1)
    fg = jnp.linspace(3.0, 6.0, MLSTM_HEADS, dtype=f32) + nrm((n_even, 2, 1, MLSTM_HEADS), 0.1)
    inp['ab_gate_bias'] = jnp.concatenate([ig, fg], axis=2).reshape(n_even, 4 * MLSTM_HEADS)
    inp['ab_conv_w'] = nrm((n_even, CONV_WIDTH, CONV_DIM), CONV_WIDTH ** -0.5)
    inp['ab_conv_b'] = nrm((n_even, CONV_DIM), 0.02)
    inp['ab_ln_g'] = gain((n_even, CONV_DIM))
    inp['ab_ln_b'] = nrm((n_even, CONV_DIM), 0.02)
    inp['ab_head_gain'] = gain((n_even, MLSTM_HEADS * MLSTM_V))
    inp['ab_w_out'] = nrm((n_even, AB_OUT, D_MODEL), AB_OUT ** -0.5)
    inp['mla_w_in'] = nrm((n_odd, D_MODEL, MLA_IN), D_MODEL ** -0.5)
    inp['mla_q_norm'] = gain((n_odd, MLA_Q_LORA))
    inp['mla_kv_norm'] = gain((n_odd, MLA_KV_LORA))
    inp['mla_w_uq'] = nrm((n_odd, MLA_Q_LORA, MLA_HEADS * MLA_QK), MLA_Q_LORA ** -0.5)
    inp['mla_w_ukv'] = nrm((n_odd, MLA_KV_LORA, MLA_HEADS * (MLA_NOPE + MLA_V)), MLA_KV_LORA ** -0.5)
    inp['mla_q_gain'] = gain((n_odd, MLA_QK))
    inp['mla_k_gain'] = gain((n_odd, MLA_QK))
    inp['mla_w_out'] = nrm((n_odd, MLA_HEADS * MLA_V, D_MODEL), (MLA_HEADS * MLA_V) ** -0.5)
    inp['ffn_w_in'] = nrm((DEPTH, D_MODEL, 2 * FFN_DIM), D_MODEL ** -0.5)
    inp['ffn_conv_w'] = nrm((DEPTH, FFN_CONV_WIDTH, FFN_DIM), FFN_CONV_WIDTH ** -0.5)
    inp['ffn_conv_b'] = nrm((DEPTH, FFN_DIM), 0.02)
    inp['ffn_w_out'] = nrm((DEPTH, FFN_DIM, D_MODEL), FFN_DIM ** -0.5)
    return inp


def reference(x, c, ctx, c_ctx, ada_w, ada_b, ab_w_in, ab_gate_bias, ab_conv_w, ab_conv_b,
              ab_ln_g, ab_ln_b, ab_head_gain, ab_w_out, mla_w_in, mla_q_norm, mla_kv_norm,
              mla_w_uq, mla_w_ukv, mla_q_gain, mla_k_gain, mla_w_out, ffn_w_in, ffn_conv_w,
              ffn_conv_b, ffn_w_out):
    rows = x.shape[1] // GRID_W
    cos, sin = _axial_rope_tables(rows)
    xl, xc = x, ctx
    for layer in range(DEPTH):
        last = layer == DEPTH - 1
        j = layer // 2
        mod_l = jnp.split((jax.nn.silu(c) @ ada_w[layer] + ada_b[layer])[:, None, :], 6, axis=-1)
        mod_c = jnp.split(jax.nn.silu(c_ctx) @ ada_w[layer] + ada_b[layer], 6, axis=-1)
        h_lat = _modulate(xl, mod_l[0], mod_l[1])
        h_ctx = _modulate(xc, mod_c[0], mod_c[1])
        if layer % 2 == 0:
            y_lat, y_ctx = _conformer_mlstm_mixer(
                h_lat, h_ctx, ab_w_in[j], ab_gate_bias[j], ab_conv_w[j], ab_conv_b[j],
                ab_ln_g[j], ab_ln_b[j], ab_head_gain[j], ab_w_out[j], not last)
        else:
            y_lat, y_ctx = _mla_mixer(
                h_lat, h_ctx, cos, sin, mla_w_in[j], mla_q_norm[j], mla_kv_norm[j], mla_w_uq[j],
                mla_w_ukv[j], mla_q_gain[j], mla_k_gain[j], mla_w_out[j], not last)
        xl = xl + mod_l[2] * y_lat
        xl = xl + mod_l[5] * _conv_ffn(_modulate(xl, mod_l[3], mod_l[4]), ffn_w_in[layer],
                                       ffn_conv_w[layer], ffn_conv_b[layer], ffn_w_out[layer])
        if not last:
            xc = xc + mod_c[2] * y_ctx
            xc = xc + mod_c[5] * _conv_ffn(_modulate(xc, mod_c[3], mod_c[4]), ffn_w_in[layer],
                                           ffn_conv_w[layer], ffn_conv_b[layer], ffn_w_out[layer])
    return xl
```

```python
import functools
import math

import jax
import jax.numpy as jnp
from jax import lax
from jax.experimental import pallas as pl
from jax.experimental.pallas import tpu as pltpu

F32 = jnp.float32
BF16 = jnp.bfloat16

EPS = 1e-6
GRID_W = 64
CONV_DIM = 512
CONV_WIDTH = 31
MLSTM_HEADS = 4
MLSTM_QK = 64
MLSTM_V = 128
MLA_HEADS = 16
MLA_Q_LORA = 512
MLA_KV_LORA = 256
MLA_NOPE = 64
MLA_ROPE = 32
MLA_V = 64
MLA_QK = MLA_NOPE + MLA_ROPE
ROPE_BASE = 10000.0
FFN_DIM = 2816

LANE = 128
HALO = 16
VMEM_LIMIT = 48 * 1024 * 1024


def _cparams(*sem):
    return pltpu.CompilerParams(dimension_semantics=sem, vmem_limit_bytes=VMEM_LIMIT)


def _const_spec(shape):
    nd = len(shape)
    return pl.BlockSpec(shape, lambda *_: (0,) * nd, pipeline_mode=pl.Buffered(1))


def _split3(a):
    hi = a.astype(BF16)
    r = a - hi.astype(F32)
    mid = r.astype(BF16)
    lo = (r - mid.astype(F32)).astype(BF16)
    return hi, mid, lo


def _rms_rows(xf):
    return xf * lax.rsqrt(jnp.mean(xf * xf, axis=-1, keepdims=True) + EPS)


def _ada_body(c_ref, w_ref, b_ref, o_ref):
    c = c_ref[...]
    s = c * jax.nn.sigmoid(c)
    s_hi, s_lo, _ = _split3(s)
    w = w_ref[0]
    w_hi = w.astype(BF16)
    w_lo = (w - w_hi.astype(F32)).astype(BF16)
    acc = jnp.dot(s_hi, w_hi, preferred_element_type=F32)
    acc += jnp.dot(s_hi, w_lo, preferred_element_type=F32)
    acc += jnp.dot(s_lo, w_hi, preferred_element_type=F32)
    o_ref[0] = acc + b_ref[0]


def _ada_mods(c8, ada_w, ada_b):
    depth, d, n = ada_w.shape
    tn = 1536
    return pl.pallas_call(
        _ada_body,
        grid=(depth, n // tn),
        in_specs=[pl.BlockSpec((8, d), lambda l, j: (0, 0)),
                  pl.BlockSpec((1, d, tn), lambda l, j: (l, 0, j)),
                  pl.BlockSpec((1, 1, tn), lambda l, j: (l, 0, j))],
        out_specs=pl.BlockSpec((1, 8, tn), lambda l, j: (l, 0, j)),
        out_shape=jax.ShapeDtypeStruct((depth, 8, n), F32),
        compiler_params=_cparams("parallel", "parallel"),
        name="ada_mods",
    )(c8, ada_w, ada_b.reshape(depth, 1, n))


def _mm_body(*refs, has_pro, has_res, n_chunk):
    it = iter(refs)
    x_ref = next(it)
    if has_pro:
        a_ref, b_ref = next(it), next(it)
    w_ref = next(it)
    if has_res:
        r_ref, g_ref = next(it), next(it)
    o_ref = next(it)
    if has_pro:
        h = (_rms_rows(x_ref[...].astype(F32)) * a_ref[0] + b_ref[0]).astype(BF16)
    else:
        h = x_ref[...].astype(BF16)
    n = o_ref.shape[-1]
    for c0 in range(0, n, n_chunk):
        c1 = min(c0 + n_chunk, n)
        acc = jnp.dot(h, w_ref[:, c0:c1], preferred_element_type=F32)
        if has_res:
            acc = r_ref[:, c0:c1] + g_ref[0][:, c0:c1] * acc
        o_ref[:, c0:c1] = acc.astype(o_ref.dtype)


def _mm(x, w, *, tm, out_dtype, group_tiles, pro=None, res=None, x_cols=None, n_chunk=512, name):
    m = x.shape[0]
    k, n = w.shape
    xcol = 0 if x_cols is None else x_cols
    in_specs = [pl.BlockSpec((tm, k), lambda i: (i, xcol))]
    args = [x]
    if pro is not None:
        in_specs += [pl.BlockSpec((1, 1, k), lambda i: (i // group_tiles, 0, 0))] * 2
        args += list(pro)
    in_specs.append(_const_spec((k, n)))
    args.append(w)
    if res is not None:
        in_specs += [pl.BlockSpec((tm, n), lambda i: (i, 0)),
                     pl.BlockSpec((1, 1, n), lambda i: (i // group_tiles, 0, 0))]
        args += list(res)
    return pl.pallas_call(
        functools.partial(_mm_body, has_pro=pro is not None, has_res=res is not None, n_chunk=n_chunk),
        grid=(m // tm,),
        in_specs=in_specs,
        out_specs=pl.BlockSpec((tm, n), lambda i: (i, 0)),
        out_shape=jax.ShapeDtypeStruct((m, n), out_dtype),
        compiler_params=_cparams("parallel"),
        name=name,
    )(*args)


def _ab_in_body(x_ref, a_ref, b_ref, w_ref, wkt_ref, wg_ref, y_ref, kt_ref, g_ref, *, n_chunk):
    hf = _rms_rows(x_ref[...]) * a_ref[0] + b_ref[0]
    h = hf.astype(BF16)
    n = y_ref.shape[-1]
    for c0 in range(0, n, n_chunk):
        c1 = min(c0 + n_chunk, n)
        y_ref[:, c0:c1] = jnp.dot(h, w_ref[:, c0:c1], preferred_element_type=F32).astype(y_ref.dtype)
    kt_ref[0] = lax.dot_general(wkt_ref[...], h, (((1,), (1,)), ((), ())),
                                preferred_element_type=F32).astype(kt_ref.dtype)
    h_hi, h_lo, _ = _split3(hf)
    acc = jnp.dot(h_hi, wg_ref[0], preferred_element_type=F32)
    acc += jnp.dot(h_hi, wg_ref[1], preferred_element_type=F32)
    acc += jnp.dot(h_lo, wg_ref[0], preferred_element_type=F32)
    g_ref[...] = acc


def _ab_in(x, mod_a, mod_b, w_main, w_kt, w_gate, *, bsz, seq, tm):
    m, d = x.shape
    n = w_main.shape[1]
    tiles = seq // tm
    return pl.pallas_call(
        functools.partial(_ab_in_body, n_chunk=512),
        grid=(m // tm,),
        in_specs=[pl.BlockSpec((tm, d), lambda i: (i, 0)),
                  pl.BlockSpec((1, 1, d), lambda i: (i // tiles, 0, 0)),
                  pl.BlockSpec((1, 1, d), lambda i: (i // tiles, 0, 0)),
                  _const_spec(w_main.shape), _const_spec(w_kt.shape), _const_spec(w_gate.shape)],
        out_specs=[pl.BlockSpec((tm, n), lambda i: (i, 0)),
                   pl.BlockSpec((1, w_kt.shape[0], tm), lambda i: (i // tiles, 0, i % tiles)),
                   pl.BlockSpec((tm, 2 * LANE), lambda i: (i, 0))],
        out_shape=[jax.ShapeDtypeStruct((m, n), BF16),
                   jax.ShapeDtypeStruct((bsz, w_kt.shape[0], seq), BF16),
                   jax.ShapeDtypeStruct((m, 2 * LANE), F32)],
        compiler_params=_cparams("parallel"),
        name="ab_in",
    )(x, mod_a, mod_b, w_main, w_kt, w_gate)


def _gate_prep_body(g_ref, bias_ref, gc_ref, gr_ref, *, chunk):
    g = g_ref[0]
    li = g[:, :LANE] + bias_ref[:, :LANE]
    lf = jax.nn.log_sigmoid(g[:, LANE:] + bias_ref[:, LANE:])
    row = lax.broadcasted_iota(jnp.int32, (chunk, chunk), 0)
    col = lax.broadcasted_iota(jnp.int32, (chunk, chunk), 1)
    lower = (col <= row).astype(BF16)
    upper = (col >= row).astype(BF16)
    b_f = jnp.zeros((chunk, LANE), F32)
    b_b = jnp.zeros((chunk, LANE), F32)
    for piece in _split3(lf):
        b_f += jnp.dot(lower, piece, preferred_element_type=F32)
        b_b += jnp.dot(upper, piece, preferred_element_type=F32)
    lane = lax.broadcasted_iota(jnp.int32, (chunk, LANE), 1)
    b = jnp.where(lane < MLSTM_HEADS, b_f, b_b)
    gc_ref[0] = b
    gr_ref[0] = jnp.transpose(li - b)[:8, :]


def _gate_prep(gates, bias, *, bsz, seq, chunk):
    g3 = gates.reshape(bsz, seq, 2 * LANE)
    nc = seq // chunk
    return pl.pallas_call(
        functools.partial(_gate_prep_body, chunk=chunk),
        grid=(bsz, nc),
        in_specs=[pl.BlockSpec((1, chunk, 2 * LANE), lambda b, c: (b, c, 0)),
                  pl.BlockSpec((1, 2 * LANE), lambda b, c: (0, 0))],
        out_specs=[pl.BlockSpec((1, chunk, LANE), lambda b, c: (b, c, 0)),
                   pl.BlockSpec((1, 8, chunk), lambda b, c: (b, 0, c))],
        out_shape=[jax.ShapeDtypeStruct((bsz, seq, LANE), F32),
                   jax.ShapeDtypeStruct((bsz, 8, seq), F32)],
        compiler_params=_cparams("parallel", "parallel"),
        name="gate_prep",
    )(g3, bias)


def _mlstm_body(qf_ref, vf_ref, ktf_ref, gcf_ref, grf_ref,
                qb_ref, vb_ref, ktb_ref, gcb_ref, grb_ref,
                s0_ref, m0_ref,
                hf_ref, hb_ref, s1_ref, m1_ref,
                s_scr, m_scr, *, chunk):
    c = pl.program_id(1)
    nc = pl.num_programs(1)

    @pl.when(c == 0)
    def _():
        s_scr[...] = s0_ref[0]
        m_scr[...] = m0_ref[0]

    row = lax.broadcasted_iota(jnp.int32, (chunk, chunk), 0)
    col = lax.broadcasted_iota(jnp.int32, (chunk, chunk), 1)
    ones_col = (lax.broadcasted_iota(jnp.int32, (chunk, LANE), 1) == 0).astype(BF16)

    for d in range(2):
        q_ref, v_ref, kt_ref, gc_ref, gr_ref, h_ref = (
            (qf_ref, vf_ref, ktf_ref, gcf_ref, grf_ref, hf_ref) if d == 0 else
            (qb_ref, vb_ref, ktb_ref, gcb_ref, grb_ref, hb_ref))
        mask = (col <= row) if d == 0 else (col >= row)
        for hd in range(MLSTM_HEADS):
            ch = d * MLSTM_HEADS + hd
            sl = slice(hd * LANE, (hd + 1) * LANE)
            q = q_ref[0][:, sl]
            kt = kt_ref[0][sl, :]
            v = v_ref[0][:, sl]
            bcol = gc_ref[0][:, ch:ch + 1]
            rrow = gr_ref[0][ch:ch + 1, :]
            m_old = m_scr[ch][0:1, 0:1]
            b_last = bcol[chunk - 1:chunk, :] if d == 0 else bcol[0:1, :]
            log_d = jnp.where(mask, bcol + rrow, -jnp.inf)
            log_inter = bcol + m_old
            m_row = jnp.maximum(log_inter, jnp.max(log_d, axis=1, keepdims=True))
            s = jnp.dot(q, kt, preferred_element_type=F32) * jnp.exp(log_d - m_row)
            inter = jnp.exp(log_inter - m_row)
            v_ext = jnp.concatenate([v, ones_col], axis=1)
            st = s_scr[ch]
            nd = (jnp.dot(s.astype(BF16), v_ext, preferred_element_type=F32)
                  + inter * jnp.dot(q, st.astype(BF16), preferred_element_type=F32))
            num = nd[:, :LANE]
            den = nd[:, LANE:LANE + 1]
            h_ref[0, :, sl] = num / jnp.maximum(jnp.abs(den), jnp.exp(-m_row))
            log_w = b_last + rrow
            m_new = jnp.maximum(b_last + m_old, jnp.max(log_w, axis=1, keepdims=True))
            w = jnp.exp(log_w - m_new)
            decay = jnp.exp(b_last + m_old - m_new)
            kw = (kt.astype(F32) * w).astype(BF16)
            s_scr[ch] = decay * st + jnp.dot(kw, v_ext, preferred_element_type=F32)
            m_scr[ch] = jnp.broadcast_to(m_new, (8, LANE))

    @pl.when(c == nc - 1)
    def _():
        s1_ref[0] = s_scr[...]
        m1_ref[0] = m_scr[...]


def _mlstm(y, kt, gc, gr, s0, m0, *, bsz, seq, chunk, q_blk, v_blk):
    nc = seq // chunk
    width = MLSTM_HEADS * LANE
    y3 = y.reshape(bsz, seq, y.shape[-1])
    nch = 2 * MLSTM_HEADS

    def fwd(b, c):
        return c

    def bwd(b, c):
        return nc - 1 - c

    def specs(pos):
        return [pl.BlockSpec((1, chunk, width), lambda b, c: (b, pos(b, c), q_blk)),
                pl.BlockSpec((1, chunk, width), lambda b, c: (b, pos(b, c), v_blk)),
                pl.BlockSpec((1, width, chunk), lambda b, c: (b, 0, pos(b, c))),
                pl.BlockSpec((1, chunk, LANE), lambda b, c: (b, pos(b, c), 0)),
                pl.BlockSpec((1, 8, chunk), lambda b, c: (b, 0, pos(b, c)))]

    state_specs = [pl.BlockSpec((1, nch, LANE, 2 * LANE), lambda b, c: (b, 0, 0, 0)),
                   pl.BlockSpec((1, nch, 8, LANE), lambda b, c: (b, 0, 0, 0))]
    return pl.pallas_call(
        functools.partial(_mlstm_body, chunk=chunk),
        grid=(bsz, nc),
        in_specs=specs(fwd) + specs(bwd) + state_specs,
        out_specs=[pl.BlockSpec((1, chunk, width), lambda b, c: (b, c, 0)),
                   pl.BlockSpec((1, chunk, width), lambda b, c: (b, nc - 1 - c, 0))] + state_specs,
        out_shape=[jax.ShapeDtypeStruct((bsz, seq, width), F32),
                   jax.ShapeDtypeStruct((bsz, seq, width), F32),
                   jax.ShapeDtypeStruct(s0.shape, F32),
                   jax.ShapeDtypeStruct(m0.shape, F32)],
        scratch_shapes=[pltpu.VMEM((nch, LANE, 2 * LANE), F32), pltpu.VMEM((nch, 8, LANE), F32)],
        compiler_params=_cparams("parallel", "arbitrary"),
        name="mlstm",
    )(y3, y3, kt, gc, gr, y3, y3, kt, gc, gr, s0, m0)


def _merge_body(ac_ref, gcur_ref, ap_ref, gp_ref, an_ref, gn_ref, o_ref, hf_ref, hb_ref,
                cw_ref, cb_ref, lg_ref, lb_ref, hg_ref, z_ref, scr, *, tm, tiles):
    i = pl.program_id(0)
    not_first = (i % tiles != 0).astype(F32)
    not_last = (i % tiles != tiles - 1).astype(F32)

    def glu(a, g):
        return a.astype(F32) * jax.nn.sigmoid(g.astype(F32))

    scr[0:HALO, :] = glu(ap_ref[...], gp_ref[...]) * not_first
    scr[HALO:HALO + tm, :] = glu(ac_ref[...], gcur_ref[...])
    scr[HALO + tm:2 * HALO + tm, :] = glu(an_ref[...], gn_ref[...]) * not_last
    pad = (CONV_WIDTH - 1) // 2
    acc = jnp.zeros((tm, CONV_DIM), F32)
    for k in range(CONV_WIDTH):
        acc += scr[pl.ds(HALO - pad + k, tm), :] * cw_ref[k:k + 1, :]
    acc += cb_ref[...]
    mu = jnp.mean(acc, axis=-1, keepdims=True)
    cen = acc - mu
    var = jnp.mean(cen * cen, axis=-1, keepdims=True)
    u = cen * lax.rsqrt(var + EPS) * lg_ref[...] + lb_ref[...]
    z_ref[:, :CONV_DIM] = (u * jax.nn.sigmoid(u)).astype(z_ref.dtype)
    hs = hf_ref[...] + hb_ref[...]
    og = hg_ref[...] * jax.nn.sigmoid(o_ref[...].astype(F32))
    for hd in range(MLSTM_HEADS):
        sl = slice(hd * MLSTM_V, (hd + 1) * MLSTM_V)
        z_ref[:, CONV_DIM + hd * MLSTM_V:CONV_DIM + (hd + 1) * MLSTM_V] = (
            _rms_rows(hs[:, sl]) * og[:, sl]).astype(z_ref.dtype)


def _merge(y, hf, hb, conv_w, conv_b, ln_g, ln_b, head_gain, *, seq, tm, a_blk, g_blk, o_blk):
    m = y.shape[0]
    tiles = seq // tm
    r = tm // HALO
    last = m // HALO - 1
    cur = lambda blk: pl.BlockSpec((tm, CONV_DIM), lambda i: (i, blk))
    prev = lambda blk: pl.BlockSpec((HALO, CONV_DIM), lambda i: (jnp.maximum(i * r - 1, 0), blk))
    nxt = lambda blk: pl.BlockSpec((HALO, CONV_DIM), lambda i: (jnp.minimum((i + 1) * r, last), blk))
    vec = lambda a: a.reshape(1, -1)
    hspec = pl.BlockSpec((tm, CONV_DIM), lambda i: (i, 0))
    return pl.pallas_call(
        functools.partial(_merge_body, tm=tm, tiles=tiles),
        grid=(m // tm,),
        in_specs=[cur(a_blk), cur(g_blk), prev(a_blk), prev(g_blk), nxt(a_blk), nxt(g_blk), cur(o_blk),
                  hspec, hspec,
                  _const_spec((CONV_WIDTH, CONV_DIM)), _const_spec((1, CONV_DIM)), _const_spec((1, CONV_DIM)),
                  _const_spec((1, CONV_DIM)), _const_spec((1, CONV_DIM))],
        out_specs=pl.BlockSpec((tm, 2 * CONV_DIM), lambda i: (i, 0)),
        out_shape=jax.ShapeDtypeStruct((m, 2 * CONV_DIM), BF16),
        scratch_shapes=[pltpu.VMEM((tm + 2 * HALO, CONV_DIM), F32)],
        compiler_params=_cparams("parallel"),
        name="ab_merge",
    )(y, y, y, y, y, y, y, hf, hb, conv_w, vec(conv_b), vec(ln_g), vec(ln_b), vec(head_gain))


def _ffn_mid_body(gc_ref, gp_ref, gn_ref, v_ref, cw_ref, cb_ref, u_ref, scr, *, tm, tiles):
    i = pl.program_id(0)
    not_first = (i % tiles != 0).astype(F32)
    not_last = (i % tiles != tiles - 1).astype(F32)
    scr[0:HALO, :] = gp_ref[...].astype(F32) * not_first
    scr[HALO:HALO + tm, :] = gc_ref[...].astype(F32)
    scr[HALO + tm:2 * HALO + tm, :] = gn_ref[...].astype(F32) * not_last
    y = (scr[pl.ds(HALO - 1, tm), :] * cw_ref[0:1, :] + scr[pl.ds(HALO, tm), :] * cw_ref[1:2, :]
         + scr[pl.ds(HALO + 1, tm), :] * cw_ref[2:3, :] + cb_ref[...])
    u_ref[...] = (jax.nn.gelu(y, approximate=True) * v_ref[...].astype(F32)).astype(u_ref.dtype)


def _ffn_mid(gv, conv_w, conv_b, *, seq, tm):
    m = gv.shape[0]
    f = conv_w.shape[1]
    tiles = seq // tm
    r = tm // HALO
    last = m // HALO - 1
    return pl.pallas_call(
        functools.partial(_ffn_mid_body, tm=tm, tiles=tiles),
        grid=(m // tm,),
        in_specs=[pl.BlockSpec((tm, f), lambda i: (i, 0)),
                  pl.BlockSpec((HALO, f), lambda i: (jnp.maximum(i * r - 1, 0), 0)),
                  pl.BlockSpec((HALO, f), lambda i: (jnp.minimum((i + 1) * r, last), 0)),
                  pl.BlockSpec((tm, f), lambda i: (i, 1)),
                  _const_spec((3, f)), _const_spec((1, f))],
        out_specs=pl.BlockSpec((tm, f), lambda i: (i, 0)),
        out_shape=jax.ShapeDtypeStruct((m, f), BF16),
        scratch_shapes=[pltpu.VMEM((tm + 2 * HALO, f), F32)],
        compiler_params=_cparams("parallel"),
        name="ffn_mid",
    )(gv, gv, gv, gv, conv_w, conv_b.reshape(1, f))


def _norm_rope(x, gain_ext, cossin):
    lane = lax.broadcasted_iota(jnp.int32, x.shape, 1)
    real = lane < MLA_QK
    ms = jnp.sum(jnp.where(real, x * x, 0.0), axis=-1, keepdims=True) * (1.0 / MLA_QK)
    t = x * lax.rsqrt(ms + EPS) * gain_ext * cossin
    rolled = pltpu.roll(t, LANE - MLA_ROPE, axis=1)
    return jnp.where(lane < MLA_NOPE, t, jnp.where(real, t + rolled, 0.0))


def _q_prep_body(cq_ref, qn_ref, w_ref, gain_ref, cs_ref, q_ref):
    h = (_rms_rows(cq_ref[...]) * qn_ref[...]).astype(BF16)
    cs = cs_ref[...]
    for hd in range(MLA_HEADS):
        x = jnp.dot(h, w_ref[:, hd * LANE:(hd + 1) * LANE], preferred_element_type=F32)
        q_ref[0, hd] = _norm_rope(x, gain_ref[...], cs).astype(q_ref.dtype)


def _q_prep(ckv, q_norm, w_uq, gain_ext, cossin, *, bsz, seq, tm):
    tiles = seq // tm
    return pl.pallas_call(
        _q_prep_body,
        grid=(bsz * tiles,),
        in_specs=[pl.BlockSpec((tm, MLA_Q_LORA), lambda i: (i, 0)),
                  _const_spec((1, MLA_Q_LORA)), _const_spec(w_uq.shape), _const_spec((1, LANE)),
                  pl.BlockSpec((tm, LANE), lambda i: (i % tiles, 0))],
        out_specs=pl.BlockSpec((1, MLA_HEADS, tm, LANE), lambda i: (i // tiles, 0, i % tiles, 0)),
        out_shape=jax.ShapeDtypeStruct((bsz, MLA_HEADS, seq, LANE), BF16),
        compiler_params=_cparams("parallel"),
        name="mla_q_prep",
    )(ckv, q_norm.reshape(1, -1), w_uq, gain_ext, cossin)


def _kv_prep_body(ckv_ref, kr_ref, kn_ref, wk_ref, wvt_ref, gain_ref, cs_ref, k_ref, vt_ref):
    h = (_rms_rows(ckv_ref[...]) * kn_ref[...]).astype(BF16)
    cs = cs_ref[...]
    kr = kr_ref[...]
    for hd in range(MLA_HEADS):
        x = jnp.dot(h, wk_ref[:, hd * LANE:(hd + 1) * LANE], preferred_element_type=F32) + kr
        k_ref[0, hd] = _norm_rope(x, gain_ref[...], cs).astype(k_ref.dtype)
    vt = lax.dot_general(wvt_ref[...], h, (((1,), (1,)), ((), ())), preferred_element_type=F32)
    for hd in range(MLA_HEADS):
        vt_ref[0, hd, 0] = vt[hd * MLA_V:(hd + 1) * MLA_V, :].astype(vt_ref.dtype)


def _kv_prep(ckv, kv_norm, w_uk, w_uvt, gain_ext, cossin, *, bsz, seq, tm):
    tiles = seq // tm
    ckv_blk = MLA_Q_LORA // MLA_KV_LORA
    kr_blk = (MLA_Q_LORA + MLA_KV_LORA) // LANE
    return pl.pallas_call(
        _kv_prep_body,
        grid=(bsz * tiles,),
        in_specs=[pl.BlockSpec((tm, MLA_KV_LORA), lambda i: (i, ckv_blk)),
                  pl.BlockSpec((tm, LANE), lambda i: (i, kr_blk)),
                  _const_spec((1, MLA_KV_LORA)), _const_spec(w_uk.shape), _const_spec(w_uvt.shape),
                  _const_spec((1, LANE)),
                  pl.BlockSpec((tm, LANE), lambda i: (i % tiles, 0))],
        out_specs=[pl.BlockSpec((1, MLA_HEADS, tm, LANE), lambda i: (i // tiles, 0, i % tiles, 0)),
                   pl.BlockSpec((1, MLA_HEADS, 1, MLA_V, tm), lambda i: (i // tiles, 0, i % tiles, 0, 0))],
        out_shape=[jax.ShapeDtypeStruct((bsz, MLA_HEADS, seq, LANE), BF16),
                   jax.ShapeDtypeStruct((bsz, MLA_HEADS, tiles, MLA_V, tm), BF16)],
        compiler_params=_cparams("parallel"),
        name="mla_kv_prep",
    )(ckv, ckv, kv_norm.reshape(1, -1), w_uk, w_uvt, gain_ext, cossin)


def _attn_body(q_ref, kc_ref, vtc_ref, kl_ref, vtl_ref, o_ref, *, n_ctx, n_lat, heads):
    tq = q_ref.shape[2]
    qs = [q_ref[0, hh] for hh in range(heads)]

    def tile(hh, kt, vt, m, l, acc):
        st = lax.dot_general(kt, qs[hh], (((1,), (1,)), ((), ())), preferred_element_type=F32)
        m_new = jnp.maximum(m, jnp.max(st, axis=0, keepdims=True))
        alpha = jnp.exp(m - m_new)
        p = jnp.exp(st - m_new)
        l = alpha * l + jnp.sum(p, axis=0, keepdims=True)
        acc = alpha * acc + jnp.dot(vt, p.astype(BF16), preferred_element_type=F32)
        return m_new, l, acc

    def step(k_ref, vt_ref):
        def body(j, carry):
            out = []
            for hh in range(heads):
                m, l, acc = carry[3 * hh:3 * hh + 3]
                out += list(tile(hh, k_ref[0, hh, j], vt_ref[0, hh, j], m, l, acc))
            return tuple(out)
        return body

    init = []
    for hh in range(heads):
        init += [jnp.full((1, tq), -1e30, F32), jnp.zeros((1, tq), F32), jnp.zeros((MLA_V, tq), F32)]
    carry = lax.fori_loop(0, n_ctx, step(kc_ref, vtc_ref), tuple(init))
    carry = lax.fori_loop(0, n_lat, step(kl_ref, vtl_ref), carry)
    outs = [carry[3 * hh + 2] / carry[3 * hh + 1] for hh in range(heads)]
    o_ref[0] = jnp.transpose(jnp.concatenate(outs, axis=0)).astype(o_ref.dtype)


def _attention(q, k_ctx, vt_ctx, k_lat, vt_lat, *, tq, tk):
    bsz, nh, seq, _ = q.shape
    heads = LANE // MLA_V
    n_ctx = k_ctx.shape[2] // tk
    n_lat = k_lat.shape[2] // tk
    k_ctx = k_ctx.reshape(bsz, nh, n_ctx, tk, LANE)
    k_lat = k_lat.reshape(bsz, nh, n_lat, tk, LANE)
    kspec = lambda n: pl.BlockSpec((1, heads, n, tk, LANE), lambda b, hp, i: (b, hp, 0, 0, 0))
    vspec = lambda n: pl.BlockSpec((1, heads, n, MLA_V, tk), lambda b, hp, i: (b, hp, 0, 0, 0))
    return pl.pallas_call(
        functools.partial(_attn_body, n_ctx=n_ctx, n_lat=n_lat, heads=heads),
        grid=(bsz, nh // heads, seq // tq),
        in_specs=[pl.BlockSpec((1, heads, tq, LANE), lambda b, hp, i: (b, hp, i, 0)),
                  kspec(n_ctx), vspec(n_ctx), kspec(n_lat), vspec(n_lat)],
        out_specs=pl.BlockSpec((1, tq, LANE), lambda b, hp, i: (b, i, hp)),
        out_shape=jax.ShapeDtypeStruct((bsz, seq, nh * MLA_V), BF16),
        compiler_params=_cparams("parallel", "parallel", "arbitrary"),
        name="mla_attention",
    )(q, k_ctx, vt_ctx, k_lat, vt_lat)


def _pad_heads(w, heads, dim):
    k = w.shape[0]
    w = w.reshape(k, heads, dim)
    return jnp.pad(w, ((0, 0), (0, 0), (0, LANE - dim))).reshape(k, heads * LANE)


def _swap_pairs(a):
    s = a.shape
    return a.reshape(*s[:-1], s[-1] // 2, 2)[..., ::-1].reshape(s)


def _ext_head(a):
    return jnp.concatenate([a, _swap_pairs(a[..., MLA_NOPE:])], axis=-1)


def _rope_table(rows):
    row = jnp.repeat(jnp.arange(rows, dtype=F32), GRID_W)
    col = jnp.tile(jnp.arange(GRID_W, dtype=F32), rows)
    half = MLA_ROPE // 2
    inv_freq = ROPE_BASE ** (-jnp.arange(0, half, 2, dtype=F32) / half)
    ang = jnp.concatenate([row[:, None] * inv_freq, col[:, None] * inv_freq], axis=-1)
    cos = jnp.repeat(jnp.cos(ang), 2, axis=-1)
    sin = jnp.repeat(jnp.sin(ang), 2, axis=-1)
    sign = jnp.tile(jnp.array([-1.0, 1.0], F32), MLA_ROPE // 2)
    return jnp.concatenate([jnp.ones((rows * GRID_W, MLA_NOPE), F32), cos, sin * sign], axis=-1)


def _no_rope_table(n):
    return jnp.concatenate([jnp.ones((n, MLA_QK), F32), jnp.zeros((n, MLA_ROPE), F32)], axis=-1)


def _row_tile(seq, want):
    t = min(want, seq)
    assert seq % t == 0 and t % HALO == 0
    return t


def kernel(x, c, ctx, c_ctx, ada_w, ada_b, ab_w_in, ab_gate_bias, ab_conv_w, ab_conv_b, ab_ln_g, ab_ln_b,
           ab_head_gain, ab_w_out, mla_w_in, mla_q_norm, mla_kv_norm, mla_w_uq, mla_w_ukv, mla_q_gain,
           mla_k_gain, mla_w_out, ffn_w_in, ffn_conv_w, ffn_conv_b, ffn_w_out):
    bsz, seq, d = x.shape
    n_ctx = ctx.shape[1]
    assert bsz <= 7 and seq % GRID_W == 0
    tm_l = _row_tile(seq, 512)
    tm_c = _row_tile(n_ctx, 512)
    chunk_l = _row_tile(seq, 256)
    chunk_c = _row_tile(n_ctx, 256)

    c8 = jnp.zeros((8, d), F32).at[:bsz].set(c).at[bsz].set(c_ctx)
    mods = _ada_mods(c8, ada_w, ada_b)

    def mod_vecs(layer):
        parts = jnp.split(mods[layer], 6, axis=-1)
        lat = [p[:bsz, None, :] for p in parts]
        cx = [jnp.broadcast_to(p[bsz][None, None, :], (bsz, 1, d)) for p in parts]
        return lat, cx

    xl = x.reshape(bsz * seq, d)
    xc = ctx.reshape(bsz * n_ctx, d)

    def conv_ffn(xr, mod, layer, seq_len, tm):
        shift, scale, gate = mod[3], mod[4], mod[5]
        tiles = seq_len // tm
        gv = _mm(xr, ffn_w_in[layer].astype(BF16), tm=tm, out_dtype=BF16, group_tiles=tiles,
                 pro=(1.0 + scale, shift), name="ffn_in")
        u = _ffn_mid(gv, ffn_conv_w[layer], ffn_conv_b[layer], seq=seq_len, tm=tm)
        return _mm(u, ffn_w_out[layer].astype(BF16), tm=tm, out_dtype=F32, group_tiles=tiles,
                   res=(xr, gate), name="ffn_out")

    lat, cx = mod_vecs(0)
    w_in = ab_w_in[0]
    wa, wg, wq, wk, wv, wo, wgt = jnp.split(
        w_in, [512, 1024, 1280, 1536, 2048, 2560], axis=1)
    w_main = jnp.concatenate([wa, wg, wv, wo, _pad_heads(wq, MLSTM_HEADS, MLSTM_QK)], axis=1).astype(BF16)
    a_blk, g_blk, v_blk, o_blk, q_blk = 0, 1, 2, 3, 4
    w_kt = jnp.transpose(_pad_heads(wk * (MLSTM_QK ** -0.5), MLSTM_HEADS, MLSTM_QK)).astype(BF16)
    wgi = jnp.concatenate([wgt[:, 0:4], wgt[:, 8:12]], axis=1)
    wgf = jnp.concatenate([wgt[:, 4:8], wgt[:, 12:16]], axis=1)
    pad8 = lambda a: jnp.pad(a, ((0, 0), (0, LANE - 8)))
    wgate = jnp.concatenate([pad8(wgi), pad8(wgf)], axis=1)
    wgate_hi = wgate.astype(BF16)
    wgate_lo = (wgate - wgate_hi.astype(F32)).astype(BF16)
    w_gate = jnp.stack([wgate_hi, wgate_lo])
    gb = ab_gate_bias[0]
    gbias = jnp.concatenate([pad8(jnp.concatenate([gb[0:4], gb[8:12]])[None, :]),
                             pad8(jnp.concatenate([gb[4:8], gb[12:16]])[None, :])], axis=1)
    w_out0 = ab_w_out[0].astype(BF16)

    def mixer0(xr, mod, seq_len, tm, chunk, s0, m0):
        y, kt, gates = _ab_in(xr, 1.0 + mod[1], mod[0], w_main, w_kt, w_gate, bsz=bsz, seq=seq_len, tm=tm)
        gc, gr = _gate_prep(gates, gbias, bsz=bsz, seq=seq_len, chunk=chunk)
        hf, hb, s1, m1 = _mlstm(y, kt, gc, gr, s0, m0, bsz=bsz, seq=seq_len, chunk=chunk,
                                q_blk=q_blk, v_blk=v_blk)
        z = _merge(y, hf.reshape(-1, hf.shape[-1]), hb.reshape(-1, hb.shape[-1]), ab_conv_w[0], ab_conv_b[0],
                   ab_ln_g[0], ab_ln_b[0], ab_head_gain[0], seq=seq_len, tm=tm,
                   a_blk=a_blk, g_blk=g_blk, o_blk=o_blk)
        out = _mm(z, w_out0, tm=tm, out_dtype=F32, group_tiles=seq_len // tm, res=(xr, mod[2]), name="ab_out")
        return out, s1, m1

    nch = 2 * MLSTM_HEADS
    s_zero = jnp.zeros((bsz, nch, LANE, 2 * LANE), F32)
    m_zero = jnp.zeros((bsz, nch, 8, LANE), F32)
    xc, s_ctx, m_ctx = mixer0(xc, cx, n_ctx, tm_c, chunk_c, s_zero, m_zero)
    xl, _, _ = mixer0(xl, lat, seq, tm_l, chunk_l, s_ctx, m_ctx)
    xl = conv_ffn(xl, lat, 0, seq, tm_l)
    xc = conv_ffn(xc, cx, 0, n_ctx, tm_c)

    lat, cx = mod_vecs(1)
    w_in = mla_w_in[0]
    w_kr = w_in[:, MLA_Q_LORA + MLA_KV_LORA:]
    w_kr_ext = jnp.concatenate([jnp.zeros((d, MLA_NOPE), F32), w_kr, _swap_pairs(w_kr)], axis=1)
    w_in1 = jnp.concatenate([w_in[:, :MLA_Q_LORA + MLA_KV_LORA], w_kr_ext], axis=1).astype(BF16)
    w_uq = mla_w_uq[0].reshape(MLA_Q_LORA, MLA_HEADS, MLA_QK)
    w_uq = _ext_head(w_uq).reshape(MLA_Q_LORA, MLA_HEADS * LANE).astype(BF16)
    w_ukv = mla_w_ukv[0].reshape(MLA_KV_LORA, MLA_HEADS, MLA_NOPE + MLA_V)
    w_uk = _pad_heads(w_ukv[..., :MLA_NOPE].reshape(MLA_KV_LORA, -1), MLA_HEADS, MLA_NOPE).astype(BF16)
    w_uvt = jnp.transpose(w_ukv[..., MLA_NOPE:].reshape(MLA_KV_LORA, -1)).astype(BF16)
    q_gain_ext = (_ext_head(mla_q_gain[0]) * (MLA_QK ** -0.5))[None, :]
    k_gain_ext = _ext_head(mla_k_gain[0])[None, :]
    rope = _rope_table(seq // GRID_W)
    no_rope = _no_rope_table(n_ctx)

    tk = 256
    assert seq % tk == 0 and n_ctx % tk == 0
    c_lat = _mm(xl, w_in1, tm=tm_l, out_dtype=F32, group_tiles=seq // tm_l,
                pro=(1.0 + lat[1], lat[0]), name="mla_in")
    c_ctx_ = _mm(xc, w_in1, tm=tm_c, out_dtype=F32, group_tiles=n_ctx // tm_c,
                 pro=(1.0 + cx[1], cx[0]), name="mla_in")
    q = _q_prep(c_lat, mla_q_norm[0], w_uq, q_gain_ext, rope, bsz=bsz, seq=seq, tm=tk)
    k_lat, vt_lat = _kv_prep(c_lat, mla_kv_norm[0], w_uk, w_uvt, k_gain_ext, rope, bsz=bsz, seq=seq, tm=tk)
    k_ctx, vt_ctx = _kv_prep(c_ctx_, mla_kv_norm[0], w_uk, w_uvt, k_gain_ext, no_rope, bsz=bsz, seq=n_ctx, tm=tk)
    att = _attention(q, k_ctx, vt_ctx, k_lat, vt_lat, tq=min(256, seq), tk=tk)
    xl = _mm(att.reshape(bsz * seq, -1), mla_w_out[0].astype(BF16), tm=tm_l, out_dtype=F32,
             group_tiles=seq // tm_l, res=(xl, lat[2]), name="mla_out")
    xl = conv_ffn(xl, lat, 1, seq, tm_l)
    return xl.reshape(bsz, seq, d)
```

```python
import functools
import math

import jax
import jax.numpy as jnp
from jax import lax
from jax.experimental import pallas as pl
from jax.experimental.pallas import tpu as pltpu

F32 = jnp.float32
BF16 = jnp.bfloat16

EPS = 1e-6
GRID_W = 64
CONV_DIM = 512
CONV_WIDTH = 31
MLSTM_HEADS = 4
MLSTM_QK = 64
MLSTM_V = 128
MLA_HEADS = 16
MLA_Q_LORA = 512
MLA_KV_LORA = 256
MLA_NOPE = 64
MLA_ROPE = 32
MLA_V = 64
MLA_QK = MLA_NOPE + MLA_ROPE
ROPE_BASE = 10000.0
FFN_DIM = 2816

LANE = 128
HALO = 16
VT_ROWS = MLA_V + 16
VMEM_LIMIT = 48 * 1024 * 1024


def _cparams(*sem):
    return pltpu.CompilerParams(dimension_semantics=sem, vmem_limit_bytes=VMEM_LIMIT)


def _const_spec(shape):
    nd = len(shape)
    return pl.BlockSpec(shape, lambda *_: (0,) * nd, pipeline_mode=pl.Buffered(1))


def _split3(a):
    hi = a.astype(BF16)
    r = a - hi.astype(F32)
    mid = r.astype(BF16)
    lo = (r - mid.astype(F32)).astype(BF16)
    return hi, mid, lo


def _rms_rows(xf):
    return xf * lax.rsqrt(jnp.mean(xf * xf, axis=-1, keepdims=True) + EPS)


def _ada_body(c_ref, w_ref, b_ref, o_ref):
    c = c_ref[...]
    s = c * jax.nn.sigmoid(c)
    s_hi, s_lo, _ = _split3(s)
    w = w_ref[0]
    w_hi = w.astype(BF16)
    w_lo = (w - w_hi.astype(F32)).astype(BF16)
    acc = jnp.dot(s_hi, w_hi, preferred_element_type=F32)
    acc += jnp.dot(s_hi, w_lo, preferred_element_type=F32)
    acc += jnp.dot(s_lo, w_hi, preferred_element_type=F32)
    o_ref[0] = acc + b_ref[0]


def _ada_mods(c8, ada_w, ada_b):
    depth, d, n = ada_w.shape
    tn = 1536
    return pl.pallas_call(
        _ada_body,
        grid=(depth, n // tn),
        in_specs=[pl.BlockSpec((8, d), lambda l, j: (0, 0)),
                  pl.BlockSpec((1, d, tn), lambda l, j: (l, 0, j)),
                  pl.BlockSpec((1, 1, tn), lambda l, j: (l, 0, j))],
        out_specs=pl.BlockSpec((1, 8, tn), lambda l, j: (l, 0, j)),
        out_shape=jax.ShapeDtypeStruct((depth, 8, n), F32),
        compiler_params=_cparams("parallel", "parallel"),
        name="ada_mods",
    )(c8, ada_w, ada_b.reshape(depth, 1, n))


def _mm_body(*refs, has_pro, has_res, n_chunk):
    it = iter(refs)
    x_ref = next(it)
    if has_pro:
        a_ref, b_ref = next(it), next(it)
    w_ref = next(it)
    if has_res:
        r_ref, g_ref = next(it), next(it)
    o_ref = next(it)
    if has_pro:
        h = (_rms_rows(x_ref[...].astype(F32)) * a_ref[0] + b_ref[0]).astype(BF16)
    else:
        h = x_ref[...].astype(BF16)
    n = o_ref.shape[-1]
    for c0 in range(0, n, n_chunk):
        c1 = min(c0 + n_chunk, n)
        acc = jnp.dot(h, w_ref[:, c0:c1], preferred_element_type=F32)
        if has_res:
            acc = r_ref[:, c0:c1] + g_ref[0][:, c0:c1] * acc
        o_ref[:, c0:c1] = acc.astype(o_ref.dtype)


def _mm(x, w, *, tm, out_dtype, group_tiles, pro=None, res=None, x_cols=None, n_chunk=512, name):
    m = x.shape[0]
    k, n = w.shape
    xcol = 0 if x_cols is None else x_cols
    in_specs = [pl.BlockSpec((tm, k), lambda i: (i, xcol))]
    args = [x]
    if pro is not None:
        in_specs += [pl.BlockSpec((1, 1, k), lambda i: (i // group_tiles, 0, 0))] * 2
        args += list(pro)
    in_specs.append(_const_spec((k, n)))
    args.append(w)
    if res is not None:
        in_specs += [pl.BlockSpec((tm, n), lambda i: (i, 0)),
                     pl.BlockSpec((1, 1, n), lambda i: (i // group_tiles, 0, 0))]
        args += list(res)
    return pl.pallas_call(
        functools.partial(_mm_body, has_pro=pro is not None, has_res=res is not None, n_chunk=n_chunk),
        grid=(m // tm,),
        in_specs=in_specs,
        out_specs=pl.BlockSpec((tm, n), lambda i: (i, 0)),
        out_shape=jax.ShapeDtypeStruct((m, n), out_dtype),
        compiler_params=_cparams("parallel"),
        name=name,
    )(*args)


def _ab_in_body(x_ref, a_ref, b_ref, w_ref, wkt_ref, wg_ref, y_ref, kt_ref, g_ref, *, n_chunk):
    hf = _rms_rows(x_ref[...]) * a_ref[0] + b_ref[0]
    h = hf.astype(BF16)
    n = y_ref.shape[-1]
    for c0 in range(0, n, n_chunk):
        c1 = min(c0 + n_chunk, n)
        y_ref[:, c0:c1] = jnp.dot(h, w_ref[:, c0:c1], preferred_element_type=F32).astype(y_ref.dtype)
    kt_ref[0] = lax.dot_general(wkt_ref[...], h, (((1,), (1,)), ((), ())),
                                preferred_element_type=F32).astype(kt_ref.dtype)
    h_hi, h_lo, _ = _split3(hf)
    acc = jnp.dot(h_hi, wg_ref[0], preferred_element_type=F32)
    acc += jnp.dot(h_hi, wg_ref[1], preferred_element_type=F32)
    acc += jnp.dot(h_lo, wg_ref[0], preferred_element_type=F32)
    g_ref[...] = acc


def _ab_in(x, mod_a, mod_b, w_main, w_kt, w_gate, *, bsz, seq, tm):
    m, d = x.shape
    n = w_main.shape[1]
    tiles = seq // tm
    return pl.pallas_call(
        functools.partial(_ab_in_body, n_chunk=512),
        grid=(m // tm,),
        in_specs=[pl.BlockSpec((tm, d), lambda i: (i, 0)),
                  pl.BlockSpec((1, 1, d), lambda i: (i // tiles, 0, 0)),
                  pl.BlockSpec((1, 1, d), lambda i: (i // tiles, 0, 0)),
                  _const_spec(w_main.shape), _const_spec(w_kt.shape), _const_spec(w_gate.shape)],
        out_specs=[pl.BlockSpec((tm, n), lambda i: (i, 0)),
                   pl.BlockSpec((1, w_kt.shape[0], tm), lambda i: (i // tiles, 0, i % tiles)),
                   pl.BlockSpec((tm, 2 * LANE), lambda i: (i, 0))],
        out_shape=[jax.ShapeDtypeStruct((m, n), BF16),
                   jax.ShapeDtypeStruct((bsz, w_kt.shape[0], seq), BF16),
                   jax.ShapeDtypeStruct((m, 2 * LANE), F32)],
        compiler_params=_cparams("parallel"),
        name="ab_in",
    )(x, mod_a, mod_b, w_main, w_kt, w_gate)


def _gate_prep_body(g_ref, bias_ref, gc_ref, gr_ref, *, chunk):
    g = g_ref[0]
    li = g[:, :LANE] + bias_ref[:, :LANE]
    lf = jax.nn.log_sigmoid(g[:, LANE:] + bias_ref[:, LANE:])
    row = lax.broadcasted_iota(jnp.int32, (chunk, chunk), 0)
    col = lax.broadcasted_iota(jnp.int32, (chunk, chunk), 1)
    lower = (col <= row).astype(BF16)
    upper = (col >= row).astype(BF16)
    b_f = jnp.zeros((chunk, LANE), F32)
    b_b = jnp.zeros((chunk, LANE), F32)
    for piece in _split3(lf):
        b_f += jnp.dot(lower, piece, preferred_element_type=F32)
        b_b += jnp.dot(upper, piece, preferred_element_type=F32)
    lane = lax.broadcasted_iota(jnp.int32, (chunk, LANE), 1)
    b = jnp.where(lane < MLSTM_HEADS, b_f, b_b)
    gc_ref[0] = b
    gr_ref[0] = jnp.transpose(li - b)[:8, :]


def _gate_prep(gates, bias, *, bsz, seq, chunk):
    g3 = gates.reshape(bsz, seq, 2 * LANE)
    nc = seq // chunk
    return pl.pallas_call(
        functools.partial(_gate_prep_body, chunk=chunk),
        grid=(bsz, nc),
        in_specs=[pl.BlockSpec((1, chunk, 2 * LANE), lambda b, c: (b, c, 0)),
                  pl.BlockSpec((1, 2 * LANE), lambda b, c: (0, 0))],
        out_specs=[pl.BlockSpec((1, chunk, LANE), lambda b, c: (b, c, 0)),
                   pl.BlockSpec((1, 8, chunk), lambda b, c: (b, 0, c))],
        out_shape=[jax.ShapeDtypeStruct((bsz, seq, LANE), F32),
                   jax.ShapeDtypeStruct((bsz, 8, seq), F32)],
        compiler_params=_cparams("parallel", "parallel"),
        name="gate_prep",
    )(g3, bias)


def _mlstm_body(qf_ref, vf_ref, ktf_ref, gcf_ref, grf_ref,
                qb_ref, vb_ref, ktb_ref, gcb_ref, grb_ref,
                s0_ref, m0_ref,
                hf_ref, hb_ref, s1_ref, m1_ref,
                s_scr, m_scr, *, chunk):
    c = pl.program_id(1)
    nc = pl.num_programs(1)

    @pl.when(c == 0)
    def _():
        s_scr[...] = s0_ref[0]
        m_scr[...] = m0_ref[0]

    row = lax.broadcasted_iota(jnp.int32, (chunk, chunk), 0)
    col = lax.broadcasted_iota(jnp.int32, (chunk, chunk), 1)
    ones_col = (lax.broadcasted_iota(jnp.int32, (chunk, LANE), 1) == 0).astype(BF16)

    for d in range(2):
        q_ref, v_ref, kt_ref, gc_ref, gr_ref, h_ref = (
            (qf_ref, vf_ref, ktf_ref, gcf_ref, grf_ref, hf_ref) if d == 0 else
            (qb_ref, vb_ref, ktb_ref, gcb_ref, grb_ref, hb_ref))
        mask = (col <= row) if d == 0 else (col >= row)
        for hd in range(MLSTM_HEADS):
            ch = d * MLSTM_HEADS + hd
            sl = slice(hd * LANE, (hd + 1) * LANE)
            q = q_ref[0][:, sl]
            kt = kt_ref[0][sl, :]
            v = v_ref[0][:, sl]
            bcol = gc_ref[0][:, ch:ch + 1]
            rrow = gr_ref[0][ch:ch + 1, :]
            m_old = m_scr[ch][0:1, 0:1]
            b_last = bcol[chunk - 1:chunk, :] if d == 0 else bcol[0:1, :]
            log_d = jnp.where(mask, bcol + rrow, -jnp.inf)
            log_inter = bcol + m_old
            m_row = jnp.maximum(log_inter, jnp.max(log_d, axis=1, keepdims=True))
            s = jnp.dot(q, kt, preferred_element_type=F32) * jnp.exp(log_d - m_row)
            inter = jnp.exp(log_inter - m_row)
            v_ext = jnp.concatenate([v, ones_col], axis=1)
            st = s_scr[ch]
            nd = (jnp.dot(s.astype(BF16), v_ext, preferred_element_type=F32)
                  + inter * jnp.dot(q, st.astype(BF16), preferred_element_type=F32))
            num = nd[:, :LANE]
            den = nd[:, LANE:LANE + 1]
            h_ref[0, :, sl] = num / jnp.maximum(jnp.abs(den), jnp.exp(-m_row))
            log_w = b_last + rrow
            m_new = jnp.maximum(b_last + m_old, jnp.max(log_w, axis=1, keepdims=True))
            w = jnp.exp(log_w - m_new)
            decay = jnp.exp(b_last + m_old - m_new)
            kw = (kt.astype(F32) * w).astype(BF16)
            s_scr[ch] = decay * st + jnp.dot(kw, v_ext, preferred_element_type=F32)
            m_scr[ch] = jnp.broadcast_to(m_new, (8, LANE))

    @pl.when(c == nc - 1)
    def _():
        s1_ref[0] = s_scr[...]
        m1_ref[0] = m_scr[...]


def _mlstm(y, kt, gc, gr, s0, m0, *, bsz, seq, chunk, q_blk, v_blk):
    nc = seq // chunk
    width = MLSTM_HEADS * LANE
    y3 = y.reshape(bsz, seq, y.shape[-1])
    nch = 2 * MLSTM_HEADS

    def fwd(b, c):
        return c

    def bwd(b, c):
        return nc - 1 - c

    def specs(pos):
        return [pl.BlockSpec((1, chunk, width), lambda b, c: (b, pos(b, c), q_blk)),
                pl.BlockSpec((1, chunk, width), lambda b, c: (b, pos(b, c), v_blk)),
                pl.BlockSpec((1, width, chunk), lambda b, c: (b, 0, pos(b, c))),
                pl.BlockSpec((1, chunk, LANE), lambda b, c: (b, pos(b, c), 0)),
                pl.BlockSpec((1, 8, chunk), lambda b, c: (b, 0, pos(b, c)))]

    state_specs = [pl.BlockSpec((1, nch, LANE, 2 * LANE), lambda b, c: (b, 0, 0, 0)),
                   pl.BlockSpec((1, nch, 8, LANE), lambda b, c: (b, 0, 0, 0))]
    return pl.pallas_call(
        functools.partial(_mlstm_body, chunk=chunk),
        grid=(bsz, nc),
        in_specs=specs(fwd) + specs(bwd) + state_specs,
        out_specs=[pl.BlockSpec((1, chunk, width), lambda b, c: (b, c, 0)),
                   pl.BlockSpec((1, chunk, width), lambda b, c: (b, nc - 1 - c, 0))] + state_specs,
        out_shape=[jax.ShapeDtypeStruct((bsz, seq, width), F32),
                   jax.ShapeDtypeStruct((bsz, seq, width), F32),
                   jax.ShapeDtypeStruct(s0.shape, F32),
                   jax.ShapeDtypeStruct(m0.shape, F32)],
        scratch_shapes=[pltpu.VMEM((nch, LANE, 2 * LANE), F32), pltpu.VMEM((nch, 8, LANE), F32)],
        compiler_params=_cparams("parallel", "arbitrary"),
        name="mlstm",
    )(y3, y3, kt, gc, gr, y3, y3, kt, gc, gr, s0, m0)


def _merge_body(ac_ref, gcur_ref, ap_ref, gp_ref, an_ref, gn_ref, o_ref, hf_ref, hb_ref,
                cw_ref, cb_ref, lg_ref, lb_ref, hg_ref, z_ref, scr, *, tm, tiles):
    i = pl.program_id(0)
    not_first = (i % tiles != 0).astype(F32)
    not_last = (i % tiles != tiles - 1).astype(F32)

    def glu(a, g):
        return a.astype(F32) * jax.nn.sigmoid(g.astype(F32))

    scr[0:HALO, :] = glu(ap_ref[...], gp_ref[...]) * not_first
    scr[HALO:HALO + tm, :] = glu(ac_ref[...], gcur_ref[...])
    scr[HALO + tm:2 * HALO + tm, :] = glu(an_ref[...], gn_ref[...]) * not_last
    pad = (CONV_WIDTH - 1) // 2
    acc = jnp.zeros((tm, CONV_DIM), F32)
    for k in range(CONV_WIDTH):
        acc += scr[pl.ds(HALO - pad + k, tm), :] * cw_ref[k:k + 1, :]
    acc += cb_ref[...]
    mu = jnp.mean(acc, axis=-1, keepdims=True)
    cen = acc - mu
    var = jnp.mean(cen * cen, axis=-1, keepdims=True)
    u = cen * lax.rsqrt(var + EPS) * lg_ref[...] + lb_ref[...]
    z_ref[:, :CONV_DIM] = (u * jax.nn.sigmoid(u)).astype(z_ref.dtype)
    hs = hf_ref[...] + hb_ref[...]
    og = hg_ref[...] * jax.nn.sigmoid(o_ref[...].astype(F32))
    for hd in range(MLSTM_HEADS):
        sl = slice(hd * MLSTM_V, (hd + 1) * MLSTM_V)
        z_ref[:, CONV_DIM + hd * MLSTM_V:CONV_DIM + (hd + 1) * MLSTM_V] = (
            _rms_rows(hs[:, sl]) * og[:, sl]).astype(z_ref.dtype)


def _merge(y, hf, hb, conv_w, conv_b, ln_g, ln_b, head_gain, *, seq, tm, a_blk, g_blk, o_blk):
    m = y.shape[0]
    tiles = seq // tm
    r = tm // HALO
    last = m // HALO - 1
    cur = lambda blk: pl.BlockSpec((tm, CONV_DIM), lambda i: (i, blk))
    prev = lambda blk: pl.BlockSpec((HALO, CONV_DIM), lambda i: (jnp.maximum(i * r - 1, 0), blk))
    nxt = lambda blk: pl.BlockSpec((HALO, CONV_DIM), lambda i: (jnp.minimum((i + 1) * r, last), blk))
    vec = lambda a: a.reshape(1, -1)
    hspec = pl.BlockSpec((tm, CONV_DIM), lambda i: (i, 0))
    return pl.pallas_call(
        functools.partial(_merge_body, tm=tm, tiles=tiles),
        grid=(m // tm,),
        in_specs=[cur(a_blk), cur(g_blk), prev(a_blk), prev(g_blk), nxt(a_blk), nxt(g_blk), cur(o_blk),
                  hspec, hspec,
                  _const_spec((CONV_WIDTH, CONV_DIM)), _const_spec((1, CONV_DIM)), _const_spec((1, CONV_DIM)),
                  _const_spec((1, CONV_DIM)), _const_spec((1, CONV_DIM))],
        out_specs=pl.BlockSpec((tm, 2 * CONV_DIM), lambda i: (i, 0)),
        out_shape=jax.ShapeDtypeStruct((m, 2 * CONV_DIM), BF16),
        scratch_shapes=[pltpu.VMEM((tm + 2 * HALO, CONV_DIM), F32)],
        compiler_params=_cparams("parallel"),
        name="ab_merge",
    )(y, y, y, y, y, y, y, hf, hb, conv_w, vec(conv_b), vec(ln_g), vec(ln_b), vec(head_gain))


def _ffn_mid_body(gc_ref, gp_ref, gn_ref, v_ref, cw_ref, cb_ref, u_ref, scr, *, tm, tiles):
    i = pl.program_id(0)
    not_first = (i % tiles != 0).astype(F32)
    not_last = (i % tiles != tiles - 1).astype(F32)
    scr[0:HALO, :] = gp_ref[...].astype(F32) * not_first
    scr[HALO:HALO + tm, :] = gc_ref[...].astype(F32)
    scr[HALO + tm:2 * HALO + tm, :] = gn_ref[...].astype(F32) * not_last
    y = (scr[pl.ds(HALO - 1, tm), :] * cw_ref[0:1, :] + scr[pl.ds(HALO, tm), :] * cw_ref[1:2, :]
         + scr[pl.ds(HALO + 1, tm), :] * cw_ref[2:3, :] + cb_ref[...])
    u_ref[...] = (jax.nn.gelu(y, approximate=True) * v_ref[...].astype(F32)).astype(u_ref.dtype)


def _ffn_mid(gv, conv_w, conv_b, *, seq, tm):
    m = gv.shape[0]
    f = conv_w.shape[1]
    tiles = seq // tm
    r = tm // HALO
    last = m // HALO - 1
    return pl.pallas_call(
        functools.partial(_ffn_mid_body, tm=tm, tiles=tiles),
        grid=(m // tm,),
        in_specs=[pl.BlockSpec((tm, f), lambda i: (i, 0)),
                  pl.BlockSpec((HALO, f), lambda i: (jnp.maximum(i * r - 1, 0), 0)),
                  pl.BlockSpec((HALO, f), lambda i: (jnp.minimum((i + 1) * r, last), 0)),
                  pl.BlockSpec((tm, f), lambda i: (i, 1)),
                  _const_spec((3, f)), _const_spec((1, f))],
        out_specs=pl.BlockSpec((tm, f), lambda i: (i, 0)),
        out_shape=jax.ShapeDtypeStruct((m, f), BF16),
        scratch_shapes=[pltpu.VMEM((tm + 2 * HALO, f), F32)],
        compiler_params=_cparams("parallel"),
        name="ffn_mid",
    )(gv, gv, gv, gv, conv_w, conv_b.reshape(1, f))


def _norm_rope(x, gain_ext, cossin):
    lane = lax.broadcasted_iota(jnp.int32, x.shape, 1)
    real = lane < MLA_QK
    ms = jnp.sum(jnp.where(real, x * x, 0.0), axis=-1, keepdims=True) * (1.0 / MLA_QK)
    t = x * lax.rsqrt(ms + EPS) * gain_ext * cossin
    rolled = pltpu.roll(t, LANE - MLA_ROPE, axis=1)
    return jnp.where(lane < MLA_NOPE, t, jnp.where(real, t + rolled, 0.0))


def _q_prep_body(cq_ref, qn_ref, w_ref, gain_ref, cs_ref, q_ref):
    h = (_rms_rows(cq_ref[...]) * qn_ref[...]).astype(BF16)
    cs = cs_ref[...]
    for hd in range(MLA_HEADS):
        x = jnp.dot(h, w_ref[:, hd * LANE:(hd + 1) * LANE], preferred_element_type=F32)
        q_ref[0, hd] = _norm_rope(x, gain_ref[...], cs).astype(q_ref.dtype)


def _q_prep(ckv, q_norm, w_uq, gain_ext, cossin, *, bsz, seq, tm):
    tiles = seq // tm
    return pl.pallas_call(
        _q_prep_body,
        grid=(bsz * tiles,),
        in_specs=[pl.BlockSpec((tm, MLA_Q_LORA), lambda i: (i, 0)),
                  _const_spec((1, MLA_Q_LORA)), _const_spec(w_uq.shape), _const_spec((1, LANE)),
                  pl.BlockSpec((tm, LANE), lambda i: (i % tiles, 0))],
        out_specs=pl.BlockSpec((1, MLA_HEADS, tm, LANE), lambda i: (i // tiles, 0, i % tiles, 0)),
        out_shape=jax.ShapeDtypeStruct((bsz, MLA_HEADS, seq, LANE), BF16),
        compiler_params=_cparams("parallel"),
        name="mla_q_prep",
    )(ckv, q_norm.reshape(1, -1), w_uq, gain_ext, cossin)


def _kv_prep_body(ckv_ref, kr_ref, kn_ref, wk_ref, wvt_ref, gain_ref, cs_ref, k_ref, vt_ref):
    h = (_rms_rows(ckv_ref[...]) * kn_ref[...]).astype(BF16)
    cs = cs_ref[...]
    kr = kr_ref[...]
    for hd in range(MLA_HEADS):
        x = jnp.dot(h, wk_ref[:, hd * LANE:(hd + 1) * LANE], preferred_element_type=F32) + kr
        k_ref[0, hd] = _norm_rope(x, gain_ref[...], cs).astype(k_ref.dtype)
    vt = lax.dot_general(wvt_ref[...], h, (((1,), (1,)), ((), ())), preferred_element_type=F32)
    tm = vt.shape[1]
    ones_rows = (lax.broadcasted_iota(jnp.int32, (VT_ROWS - MLA_V, tm), 0) == 0).astype(vt_ref.dtype)
    for hd in range(MLA_HEADS):
        vt_ref[0, hd, 0, :MLA_V, :] = vt[hd * MLA_V:(hd + 1) * MLA_V, :].astype(vt_ref.dtype)
        vt_ref[0, hd, 0, MLA_V:, :] = ones_rows


def _kv_prep(ckv, kv_norm, w_uk, w_uvt, gain_ext, cossin, *, bsz, seq, tm, vt_width):
    tiles = seq // tm
    sub = vt_width // tm
    ckv_blk = MLA_Q_LORA // MLA_KV_LORA
    kr_blk = (MLA_Q_LORA + MLA_KV_LORA) // LANE
    return pl.pallas_call(
        _kv_prep_body,
        grid=(bsz * tiles,),
        in_specs=[pl.BlockSpec((tm, MLA_KV_LORA), lambda i: (i, ckv_blk)),
                  pl.BlockSpec((tm, LANE), lambda i: (i, kr_blk)),
                  _const_spec((1, MLA_KV_LORA)), _const_spec(w_uk.shape), _const_spec(w_uvt.shape),
                  _const_spec((1, LANE)),
                  pl.BlockSpec((tm, LANE), lambda i: (i % tiles, 0))],
        out_specs=[pl.BlockSpec((1, MLA_HEADS, tm, LANE), lambda i: (i // tiles, 0, i % tiles, 0)),
                   pl.BlockSpec((1, MLA_HEADS, 1, VT_ROWS, tm),
                                lambda i: (i // tiles, 0, (i % tiles) // sub, 0, (i % tiles) % sub))],
        out_shape=[jax.ShapeDtypeStruct((bsz, MLA_HEADS, seq, LANE), BF16),
                   jax.ShapeDtypeStruct((bsz, MLA_HEADS, seq // vt_width, VT_ROWS, vt_width), BF16)],
        compiler_params=_cparams("parallel"),
        name="mla_kv_prep",
    )(ckv, ckv, kv_norm.reshape(1, -1), w_uk, w_uvt, gain_ext, cossin)


def _attn_body(q_ref, kc_ref, vtc_ref, kl_ref, vtl_ref, o_ref, s_a, s_b, p_a, p_b, *, n_lat, heads):
    qs = [q_ref[0, hh] for hh in range(heads)]

    def scores(kt, hh):
        return lax.dot_general(kt, qs[hh], (((1,), (1,)), ((), ())), preferred_element_type=F32)

    carry = []
    for hh in range(heads):
        st = scores(kc_ref[0, hh], hh)
        m = jnp.max(st, axis=0, keepdims=True)
        acc = jnp.dot(vtc_ref[0, hh, 0], jnp.exp2(st - m).astype(BF16), preferred_element_type=F32)
        st = scores(kl_ref[0, hh, 0], hh)
        s_a[hh] = st
        p_b[hh] = jnp.zeros(p_b.shape[1:], p_b.dtype)
        carry += [m, jnp.max(st, axis=0, keepdims=True), acc]

    def block_step(j, s_cur, s_nxt, p_wr, p_rd, carry):
        out = []
        jn = jnp.minimum(j + 1, n_lat - 1)
        jp = jnp.maximum(j - 1, 0)
        for hh in range(heads):
            m, mb, acc = carry[3 * hh:3 * hh + 3]
            st_n = scores(kl_ref[0, hh, jn], hh)
            s_nxt[hh] = st_n
            mb_n = jnp.max(st_n, axis=0, keepdims=True)
            pv = jnp.dot(vtl_ref[0, hh, jp], p_rd[hh], preferred_element_type=F32)
            m_new = jnp.maximum(m, mb)
            p_wr[hh] = jnp.exp2(s_cur[hh] - m_new).astype(BF16)
            out += [m_new, mb_n, jnp.exp2(m - m_new) * (acc + pv)]
        return tuple(out)

    def pair(t, carry):
        carry = block_step(2 * t, s_a, s_b, p_a, p_b, carry)
        return block_step(2 * t + 1, s_b, s_a, p_b, p_a, carry)

    carry = lax.fori_loop(0, n_lat // 2, pair, tuple(carry))
    outs = []
    for hh in range(heads):
        acc = carry[3 * hh + 2] + jnp.dot(vtl_ref[0, hh, n_lat - 1], p_b[hh], preferred_element_type=F32)
        outs.append(acc[:MLA_V] / acc[MLA_V:MLA_V + 1])
    o_ref[0] = jnp.transpose(jnp.concatenate(outs, axis=0)).astype(o_ref.dtype)


def _attention(q, k_ctx, vt_ctx, k_lat, vt_lat, *, tq):
    bsz, nh, seq, _ = q.shape
    heads = LANE // MLA_V
    n_c = k_ctx.shape[2]
    n_lat, _, kb = vt_lat.shape[2:]
    assert n_lat % 2 == 0
    k_lat = k_lat.reshape(bsz, nh, n_lat, kb, LANE)
    return pl.pallas_call(
        functools.partial(_attn_body, n_lat=n_lat, heads=heads),
        grid=(bsz, nh // heads, seq // tq),
        in_specs=[pl.BlockSpec((1, heads, tq, LANE), lambda b, hp, i: (b, hp, i, 0)),
                  pl.BlockSpec((1, heads, n_c, LANE), lambda b, hp, i: (b, hp, 0, 0)),
                  pl.BlockSpec((1, heads, 1, VT_ROWS, n_c), lambda b, hp, i: (b, hp, 0, 0, 0)),
                  pl.BlockSpec((1, heads, n_lat, kb, LANE), lambda b, hp, i: (b, hp, 0, 0, 0)),
                  pl.BlockSpec((1, heads, n_lat, VT_ROWS, kb), lambda b, hp, i: (b, hp, 0, 0, 0))],
        out_specs=pl.BlockSpec((1, tq, LANE), lambda b, hp, i: (b, i, hp)),
        out_shape=jax.ShapeDtypeStruct((bsz, seq, nh * MLA_V), BF16),
        scratch_shapes=[pltpu.VMEM((heads, kb, tq), F32), pltpu.VMEM((heads, kb, tq), F32),
                        pltpu.VMEM((heads, kb, tq), BF16), pltpu.VMEM((heads, kb, tq), BF16)],
        compiler_params=_cparams("parallel", "parallel", "arbitrary"),
        name="mla_attention",
    )(q, k_ctx, vt_ctx, k_lat, vt_lat)


def _pad_heads(w, heads, dim):
    k = w.shape[0]
    w = w.reshape(k, heads, dim)
    return jnp.pad(w, ((0, 0), (0, 0), (0, LANE - dim))).reshape(k, heads * LANE)


def _swap_pairs(a):
    s = a.shape
    return a.reshape(*s[:-1], s[-1] // 2, 2)[..., ::-1].reshape(s)


def _ext_head(a):
    return jnp.concatenate([a, _swap_pairs(a[..., MLA_NOPE:])], axis=-1)


def _rope_table(rows):
    row = jnp.repeat(jnp.arange(rows, dtype=F32), GRID_W)
    col = jnp.tile(jnp.arange(GRID_W, dtype=F32), rows)
    half = MLA_ROPE // 2
    inv_freq = ROPE_BASE ** (-jnp.arange(0, half, 2, dtype=F32) / half)
    ang = jnp.concatenate([row[:, None] * inv_freq, col[:, None] * inv_freq], axis=-1)
    cos = jnp.repeat(jnp.cos(ang), 2, axis=-1)
    sin = jnp.repeat(jnp.sin(ang), 2, axis=-1)
    sign = jnp.tile(jnp.array([-1.0, 1.0], F32), MLA_ROPE // 2)
    return jnp.concatenate([jnp.ones((rows * GRID_W, MLA_NOPE), F32), cos, sin * sign], axis=-1)


def _no_rope_table(n):
    return jnp.concatenate([jnp.ones((n, MLA_QK), F32), jnp.zeros((n, MLA_ROPE), F32)], axis=-1)


def _row_tile(seq, want):
    t = min(want, seq)
    assert seq % t == 0 and t % HALO == 0
    return t


def kernel(x, c, ctx, c_ctx, ada_w, ada_b, ab_w_in, ab_gate_bias, ab_conv_w, ab_conv_b, ab_ln_g, ab_ln_b,
           ab_head_gain, ab_w_out, mla_w_in, mla_q_norm, mla_kv_norm, mla_w_uq, mla_w_ukv, mla_q_gain,
           mla_k_gain, mla_w_out, ffn_w_in, ffn_conv_w, ffn_conv_b, ffn_w_out):
    bsz, seq, d = x.shape
    n_ctx = ctx.shape[1]
    assert bsz <= 7 and seq % GRID_W == 0
    tm_l = _row_tile(seq, 512)
    tm_c = _row_tile(n_ctx, 512)
    chunk_l = _row_tile(seq, 256)
    chunk_c = _row_tile(n_ctx, 256)

    c8 = jnp.zeros((8, d), F32).at[:bsz].set(c).at[bsz].set(c_ctx)
    mods = _ada_mods(c8, ada_w, ada_b)

    def mod_vecs(layer):
        parts = jnp.split(mods[layer], 6, axis=-1)
        lat = [p[:bsz, None, :] for p in parts]
        cx = [jnp.broadcast_to(p[bsz][None, None, :], (bsz, 1, d)) for p in parts]
        return lat, cx

    xl = x.reshape(bsz * seq, d)
    xc = ctx.reshape(bsz * n_ctx, d)

    def conv_ffn(xr, mod, layer, seq_len, tm):
        shift, scale, gate = mod[3], mod[4], mod[5]
        tiles = seq_len // tm
        gv = _mm(xr, ffn_w_in[layer].astype(BF16), tm=tm, out_dtype=BF16, group_tiles=tiles,
                 pro=(1.0 + scale, shift), name="ffn_in")
        u = _ffn_mid(gv, ffn_conv_w[layer], ffn_conv_b[layer], seq=seq_len, tm=tm)
        return _mm(u, ffn_w_out[layer].astype(BF16), tm=tm, out_dtype=F32, group_tiles=tiles,
                   res=(xr, gate), name="ffn_out")

    lat, cx = mod_vecs(0)
    w_in = ab_w_in[0]
    wa, wg, wq, wk, wv, wo, wgt = jnp.split(
        w_in, [512, 1024, 1280, 1536, 2048, 2560], axis=1)
    w_main = jnp.concatenate([wa, wg, wv, wo, _pad_heads(wq, MLSTM_HEADS, MLSTM_QK)], axis=1).astype(BF16)
    a_blk, g_blk, v_blk, o_blk, q_blk = 0, 1, 2, 3, 4
    w_kt = jnp.transpose(_pad_heads(wk * (MLSTM_QK ** -0.5), MLSTM_HEADS, MLSTM_QK)).astype(BF16)
    wgi = jnp.concatenate([wgt[:, 0:4], wgt[:, 8:12]], axis=1)
    wgf = jnp.concatenate([wgt[:, 4:8], wgt[:, 12:16]], axis=1)
    pad8 = lambda a: jnp.pad(a, ((0, 0), (0, LANE - 8)))
    wgate = jnp.concatenate([pad8(wgi), pad8(wgf)], axis=1)
    wgate_hi = wgate.astype(BF16)
    wgate_lo = (wgate - wgate_hi.astype(F32)).astype(BF16)
    w_gate = jnp.stack([wgate_hi, wgate_lo])
    gb = ab_gate_bias[0]
    gbias = jnp.concatenate([pad8(jnp.concatenate([gb[0:4], gb[8:12]])[None, :]),
                             pad8(jnp.concatenate([gb[4:8], gb[12:16]])[None, :])], axis=1)
    w_out0 = ab_w_out[0].astype(BF16)

    def mixer0(xr, mod, seq_len, tm, chunk, s0, m0):
        y, kt, gates = _ab_in(xr, 1.0 + mod[1], mod[0], w_main, w_kt, w_gate, bsz=bsz, seq=seq_len, tm=tm)
        gc, gr = _gate_prep(gates, gbias, bsz=bsz, seq=seq_len, chunk=chunk)
        hf, hb, s1, m1 = _mlstm(y, kt, gc, gr, s0, m0, bsz=bsz, seq=seq_len, chunk=chunk,
                                q_blk=q_blk, v_blk=v_blk)
        z = _merge(y, hf.reshape(-1, hf.shape[-1]), hb.reshape(-1, hb.shape[-1]), ab_conv_w[0], ab_conv_b[0],
                   ab_ln_g[0], ab_ln_b[0], ab_head_gain[0], seq=seq_len, tm=tm,
                   a_blk=a_blk, g_blk=g_blk, o_blk=o_blk)
        out = _mm(z, w_out0, tm=tm, out_dtype=F32, group_tiles=seq_len // tm, res=(xr, mod[2]), name="ab_out")
        return out, s1, m1

    nch = 2 * MLSTM_HEADS
    s_zero = jnp.zeros((bsz, nch, LANE, 2 * LANE), F32)
    m_zero = jnp.zeros((bsz, nch, 8, LANE), F32)
    xc, s_ctx, m_ctx = mixer0(xc, cx, n_ctx, tm_c, chunk_c, s_zero, m_zero)
    xl, _, _ = mixer0(xl, lat, seq, tm_l, chunk_l, s_ctx, m_ctx)
    xl = conv_ffn(xl, lat, 0, seq, tm_l)
    xc = conv_ffn(xc, cx, 0, n_ctx, tm_c)

    lat, cx = mod_vecs(1)
    w_in = mla_w_in[0]
    w_kr = w_in[:, MLA_Q_LORA + MLA_KV_LORA:]
    w_kr_ext = jnp.concatenate([jnp.zeros((d, MLA_NOPE), F32), w_kr, _swap_pairs(w_kr)], axis=1)
    w_in1 = jnp.concatenate([w_in[:, :MLA_Q_LORA + MLA_KV_LORA], w_kr_ext], axis=1).astype(BF16)
    w_uq = mla_w_uq[0].reshape(MLA_Q_LORA, MLA_HEADS, MLA_QK)
    w_uq = _ext_head(w_uq).reshape(MLA_Q_LORA, MLA_HEADS * LANE).astype(BF16)
    w_ukv = mla_w_ukv[0].reshape(MLA_KV_LORA, MLA_HEADS, MLA_NOPE + MLA_V)
    w_uk = _pad_heads(w_ukv[..., :MLA_NOPE].reshape(MLA_KV_LORA, -1), MLA_HEADS, MLA_NOPE).astype(BF16)
    w_uvt = jnp.transpose(w_ukv[..., MLA_NOPE:].reshape(MLA_KV_LORA, -1)).astype(BF16)
    q_gain_ext = (_ext_head(mla_q_gain[0]) * (MLA_QK ** -0.5 * math.log2(math.e)))[None, :]
    k_gain_ext = _ext_head(mla_k_gain[0])[None, :]
    rope = _rope_table(seq // GRID_W)
    no_rope = _no_rope_table(n_ctx)

    tk = 256
    kv_blk = min(512, seq // 2)
    assert seq % kv_blk == 0 and kv_blk % tk == 0 and n_ctx % tk == 0
    c_lat = _mm(xl, w_in1, tm=tm_l, out_dtype=F32, group_tiles=seq // tm_l,
                pro=(1.0 + lat[1], lat[0]), name="mla_in")
    c_ctx_ = _mm(xc, w_in1, tm=tm_c, out_dtype=F32, group_tiles=n_ctx // tm_c,
                 pro=(1.0 + cx[1], cx[0]), name="mla_in")
    q = _q_prep(c_lat, mla_q_norm[0], w_uq, q_gain_ext, rope, bsz=bsz, seq=seq, tm=tk)
    k_lat, vt_lat = _kv_prep(c_lat, mla_kv_norm[0], w_uk, w_uvt, k_gain_ext, rope, bsz=bsz, seq=seq, tm=tk,
                             vt_width=kv_blk)
    k_ctx, vt_ctx = _kv_prep(c_ctx_, mla_kv_norm[0], w_uk, w_uvt, k_gain_ext, no_rope, bsz=bsz, seq=n_ctx, tm=tk,
                             vt_width=n_ctx)
    att = _attention(q, k_ctx, vt_ctx, k_lat, vt_lat, tq=min(256, seq))
    xl = _mm(att.reshape(bsz * seq, -1), mla_w_out[0].astype(BF16), tm=tm_l, out_dtype=F32,
             group_tiles=seq // tm_l, res=(xl, lat[2]), name="mla_out")
    xl = conv_ffn(xl, lat, 1, seq, tm_l)
    return xl.reshape(bsz, seq, d)
```

```python
import functools
import math

import jax
import jax.numpy as jnp
from jax import lax
from jax.experimental import pallas as pl
from jax.experimental.pallas import tpu as pltpu

F32 = jnp.float32
BF16 = jnp.bfloat16

EPS = 1e-6
GRID_W = 64
CONV_DIM = 512
CONV_WIDTH = 31
MLSTM_HEADS = 4
MLSTM_QK = 64
MLSTM_V = 128
MLA_HEADS = 16
MLA_Q_LORA = 512
MLA_KV_LORA = 256
MLA_NOPE = 64
MLA_ROPE = 32
MLA_V = 64
MLA_QK = MLA_NOPE + MLA_ROPE
ROPE_BASE = 10000.0
FFN_DIM = 2816

LANE = 128
SUBLANE = 8
HALO = 16
CONV_ROWS = 64
VT_ROWS = MLA_V + 16
VMEM_LIMIT = 48 * 1024 * 1024


def _cparams(*sem):
    return pltpu.CompilerParams(dimension_semantics=sem, vmem_limit_bytes=VMEM_LIMIT)


def _const_spec(shape):
    nd = len(shape)
    return pl.BlockSpec(shape, lambda *_: (0,) * nd, pipeline_mode=pl.Buffered(1))


def _split3(a):
    hi = a.astype(BF16)
    r = a - hi.astype(F32)
    mid = r.astype(BF16)
    lo = (r - mid.astype(F32)).astype(BF16)
    return hi, mid, lo


def _rms_rows(xf):
    return xf * lax.rsqrt(jnp.mean(xf * xf, axis=-1, keepdims=True) + EPS)


def _ada_body(c_ref, w_ref, b_ref, o_ref):
    c = c_ref[...]
    s = c * jax.nn.sigmoid(c)
    s_hi, s_lo, _ = _split3(s)
    w = w_ref[0]
    w_hi = w.astype(BF16)
    w_lo = (w - w_hi.astype(F32)).astype(BF16)
    acc = jnp.dot(s_hi, w_hi, preferred_element_type=F32)
    acc += jnp.dot(s_hi, w_lo, preferred_element_type=F32)
    acc += jnp.dot(s_lo, w_hi, preferred_element_type=F32)
    o_ref[0] = acc + b_ref[0]


def _ada_mods(c8, ada_w, ada_b):
    depth, d, n = ada_w.shape
    tn = 1536
    return pl.pallas_call(
        _ada_body,
        grid=(depth, n // tn),
        in_specs=[pl.BlockSpec((8, d), lambda l, j: (0, 0)),
                  pl.BlockSpec((1, d, tn), lambda l, j: (l, 0, j)),
                  pl.BlockSpec((1, 1, tn), lambda l, j: (l, 0, j))],
        out_specs=pl.BlockSpec((1, 8, tn), lambda l, j: (l, 0, j)),
        out_shape=jax.ShapeDtypeStruct((depth, 8, n), F32),
        compiler_params=_cparams("parallel", "parallel"),
        name="ada_mods",
    )(c8, ada_w, ada_b.reshape(depth, 1, n))


def _mm_body(*refs, has_pro, has_res, n_chunk):
    it = iter(refs)
    x_ref = next(it)
    if has_pro:
        a_ref, b_ref = next(it), next(it)
    w_ref = next(it)
    if has_res:
        r_ref, g_ref = next(it), next(it)
    o_ref = next(it)
    if has_pro:
        h = (_rms_rows(x_ref[...].astype(F32)) * a_ref[0] + b_ref[0]).astype(BF16)
    else:
        h = x_ref[...].astype(BF16)
    n = o_ref.shape[-1]
    for c0 in range(0, n, n_chunk):
        c1 = min(c0 + n_chunk, n)
        acc = jnp.dot(h, w_ref[:, c0:c1], preferred_element_type=F32)
        if has_res:
            acc = r_ref[:, c0:c1] + g_ref[0][:, c0:c1] * acc
        o_ref[:, c0:c1] = acc.astype(o_ref.dtype)


def _mm(x, w, *, tm, out_dtype, group_tiles, pro=None, res=None, x_cols=None, n_chunk=512, name):
    m = x.shape[0]
    k, n = w.shape
    xcol = 0 if x_cols is None else x_cols
    in_specs = [pl.BlockSpec((tm, k), lambda i: (i, xcol))]
    args = [x]
    if pro is not None:
        in_specs += [pl.BlockSpec((1, 1, k), lambda i: (i // group_tiles, 0, 0))] * 2
        args += list(pro)
    in_specs.append(_const_spec((k, n)))
    args.append(w)
    if res is not None:
        in_specs += [pl.BlockSpec((tm, n), lambda i: (i, 0)),
                     pl.BlockSpec((1, 1, n), lambda i: (i // group_tiles, 0, 0))]
        args += list(res)
    return pl.pallas_call(
        functools.partial(_mm_body, has_pro=pro is not None, has_res=res is not None, n_chunk=n_chunk),
        grid=(m // tm,),
        in_specs=in_specs,
        out_specs=pl.BlockSpec((tm, n), lambda i: (i, 0)),
        out_shape=jax.ShapeDtypeStruct((m, n), out_dtype),
        compiler_params=_cparams("parallel"),
        name=name,
    )(*args)


def _ab_in_body(x_ref, a_ref, b_ref, w_ref, wkt_ref, wg_ref, y_ref, kt_ref, g_ref, *, n_chunk):
    hf = _rms_rows(x_ref[...]) * a_ref[0] + b_ref[0]
    h = hf.astype(BF16)
    n = y_ref.shape[-1]
    for c0 in range(0, n, n_chunk):
        c1 = min(c0 + n_chunk, n)
        y_ref[:, c0:c1] = jnp.dot(h, w_ref[:, c0:c1], preferred_element_type=F32).astype(y_ref.dtype)
    kt_ref[0] = lax.dot_general(wkt_ref[...], h, (((1,), (1,)), ((), ())),
                                preferred_element_type=F32).astype(kt_ref.dtype)
    h_hi, h_lo, _ = _split3(hf)
    acc = jnp.dot(h_hi, wg_ref[0], preferred_element_type=F32)
    acc += jnp.dot(h_hi, wg_ref[1], preferred_element_type=F32)
    acc += jnp.dot(h_lo, wg_ref[0], preferred_element_type=F32)
    g_ref[...] = acc


def _ab_in(x, mod_a, mod_b, w_main, w_kt, w_gate, *, bsz, seq, tm):
    m, d = x.shape
    n = w_main.shape[1]
    tiles = seq // tm
    return pl.pallas_call(
        functools.partial(_ab_in_body, n_chunk=512),
        grid=(m // tm,),
        in_specs=[pl.BlockSpec((tm, d), lambda i: (i, 0)),
                  pl.BlockSpec((1, 1, d), lambda i: (i // tiles, 0, 0)),
                  pl.BlockSpec((1, 1, d), lambda i: (i // tiles, 0, 0)),
                  _const_spec(w_main.shape), _const_spec(w_kt.shape), _const_spec(w_gate.shape)],
        out_specs=[pl.BlockSpec((tm, n), lambda i: (i, 0)),
                   pl.BlockSpec((1, w_kt.shape[0], tm), lambda i: (i // tiles, 0, i % tiles)),
                   pl.BlockSpec((tm, 2 * LANE), lambda i: (i, 0))],
        out_shape=[jax.ShapeDtypeStruct((m, n), BF16),
                   jax.ShapeDtypeStruct((bsz, w_kt.shape[0], seq), BF16),
                   jax.ShapeDtypeStruct((m, 2 * LANE), F32)],
        compiler_params=_cparams("parallel"),
        name="ab_in",
    )(x, mod_a, mod_b, w_main, w_kt, w_gate)


def _gate_prep_body(g_ref, bias_ref, gc_ref, gr_ref, *, chunk):
    g = g_ref[0]
    li = g[:, :LANE] + bias_ref[:, :LANE]
    lf = jax.nn.log_sigmoid(g[:, LANE:] + bias_ref[:, LANE:])
    row = lax.broadcasted_iota(jnp.int32, (chunk, chunk), 0)
    col = lax.broadcasted_iota(jnp.int32, (chunk, chunk), 1)
    lower = (col <= row).astype(BF16)
    upper = (col >= row).astype(BF16)
    b_f = jnp.zeros((chunk, LANE), F32)
    b_b = jnp.zeros((chunk, LANE), F32)
    for piece in _split3(lf):
        b_f += jnp.dot(lower, piece, preferred_element_type=F32)
        b_b += jnp.dot(upper, piece, preferred_element_type=F32)
    lane = lax.broadcasted_iota(jnp.int32, (chunk, LANE), 1)
    fwd = lane < MLSTM_HEADS
    b = jnp.where(fwd, b_f, b_b)
    r = li - b
    tok = lax.broadcasted_iota(jnp.int32, (chunk, LANE), 0)
    rm_f, rm_b = r, r
    step = 1
    while step < chunk:
        rm_f = jnp.maximum(rm_f, jnp.where(tok >= step, pltpu.roll(rm_f, step, axis=0), -jnp.inf))
        rm_b = jnp.maximum(rm_b, jnp.where(tok < chunk - step, pltpu.roll(rm_b, chunk - step, axis=0), -jnp.inf))
        step *= 2
    gc_ref[0, :, :LANE] = b
    gc_ref[0, :, LANE:] = jnp.where(fwd, rm_f, rm_b)
    gr_ref[0] = jnp.transpose(r)[:8, :]


def _gate_prep(gates, bias, *, bsz, seq, chunk):
    g3 = gates.reshape(bsz, seq, 2 * LANE)
    nc = seq // chunk
    return pl.pallas_call(
        functools.partial(_gate_prep_body, chunk=chunk),
        grid=(bsz, nc),
        in_specs=[pl.BlockSpec((1, chunk, 2 * LANE), lambda b, c: (b, c, 0)),
                  pl.BlockSpec((1, 2 * LANE), lambda b, c: (0, 0))],
        out_specs=[pl.BlockSpec((1, chunk, 2 * LANE), lambda b, c: (b, c, 0)),
                   pl.BlockSpec((1, 8, chunk), lambda b, c: (b, 0, c))],
        out_shape=[jax.ShapeDtypeStruct((bsz, seq, 2 * LANE), F32),
                   jax.ShapeDtypeStruct((bsz, 8, seq), F32)],
        compiler_params=_cparams("parallel", "parallel"),
        name="gate_prep",
    )(g3, bias)


def _mlstm_body(qf_ref, vf_ref, ktf_ref, gcf_ref, grf_ref,
                qb_ref, vb_ref, ktb_ref, gcb_ref, grb_ref,
                s0_ref, m0_ref,
                hf_ref, hb_ref, s1_ref, m1_ref,
                s_scr, m_scr, *, chunk):
    c = pl.program_id(1)
    nc = pl.num_programs(1)

    @pl.when(c == 0)
    def _():
        s_scr[...] = s0_ref[0]
        m_scr[...] = m0_ref[0]

    row = lax.broadcasted_iota(jnp.int32, (chunk, chunk), 0)
    col = lax.broadcasted_iota(jnp.int32, (chunk, chunk), 1)
    ones_col = (lax.broadcasted_iota(jnp.int32, (chunk, LANE), 1) == 0).astype(BF16)

    for d in range(2):
        q_ref, v_ref, kt_ref, gc_ref, gr_ref, h_ref = (
            (qf_ref, vf_ref, ktf_ref, gcf_ref, grf_ref, hf_ref) if d == 0 else
            (qb_ref, vb_ref, ktb_ref, gcb_ref, grb_ref, hb_ref))
        mask = (col <= row) if d == 0 else (col >= row)
        for hd in range(MLSTM_HEADS):
            ch = d * MLSTM_HEADS + hd
            sl = slice(hd * LANE, (hd + 1) * LANE)
            q = q_ref[0][:, sl]
            kt = kt_ref[0][sl, :]
            v = v_ref[0][:, sl]
            bcol = gc_ref[0][:, ch:ch + 1]
            rmcol = gc_ref[0][:, LANE + ch:LANE + ch + 1]
            rrow = gr_ref[0][ch:ch + 1, :]
            m_old = m_scr[ch][0:1, 0:1]
            b_last = bcol[chunk - 1:chunk, :] if d == 0 else bcol[0:1, :]
            cm = jnp.maximum(rmcol, m_old)
            s = jnp.dot(q, kt, preferred_element_type=F32) * jnp.exp(jnp.where(mask, rrow - cm, -jnp.inf))
            inter = jnp.exp(m_old - cm)
            v_ext = jnp.concatenate([v, ones_col], axis=1)
            st = s_scr[ch]
            nd = (jnp.dot(s.astype(BF16), v_ext, preferred_element_type=F32)
                  + inter * jnp.dot(q, st.astype(BF16), preferred_element_type=F32))
            num = nd[:, :LANE]
            den = nd[:, LANE:LANE + 1]
            h_ref[0, :, sl] = num * (1.0 / jnp.maximum(jnp.abs(den), jnp.exp(-(bcol + cm))))
            log_w = b_last + rrow
            m_new = jnp.maximum(b_last + m_old, jnp.max(log_w, axis=1, keepdims=True))
            w = jnp.exp(log_w - m_new)
            decay = jnp.exp(b_last + m_old - m_new)
            kw = (kt.astype(F32) * w).astype(BF16)
            s_scr[ch] = decay * st + jnp.dot(kw, v_ext, preferred_element_type=F32)
            m_scr[ch] = jnp.broadcast_to(m_new, (8, LANE))

    @pl.when(c == nc - 1)
    def _():
        s1_ref[0] = s_scr[...]
        m1_ref[0] = m_scr[...]


def _mlstm(y, kt, gc, gr, s0, m0, *, bsz, seq, chunk, q_blk, v_blk):
    nc = seq // chunk
    width = MLSTM_HEADS * LANE
    y3 = y.reshape(bsz, seq, y.shape[-1])
    nch = 2 * MLSTM_HEADS

    def fwd(b, c):
        return c

    def bwd(b, c):
        return nc - 1 - c

    def specs(pos):
        return [pl.BlockSpec((1, chunk, width), lambda b, c: (b, pos(b, c), q_blk)),
                pl.BlockSpec((1, chunk, width), lambda b, c: (b, pos(b, c), v_blk)),
                pl.BlockSpec((1, width, chunk), lambda b, c: (b, 0, pos(b, c))),
                pl.BlockSpec((1, chunk, 2 * LANE), lambda b, c: (b, pos(b, c), 0)),
                pl.BlockSpec((1, 8, chunk), lambda b, c: (b, 0, pos(b, c)))]

    state_specs = [pl.BlockSpec((1, nch, LANE, 2 * LANE), lambda b, c: (b, 0, 0, 0)),
                   pl.BlockSpec((1, nch, 8, LANE), lambda b, c: (b, 0, 0, 0))]
    return pl.pallas_call(
        functools.partial(_mlstm_body, chunk=chunk),
        grid=(bsz, nc),
        in_specs=specs(fwd) + specs(bwd) + state_specs,
        out_specs=[pl.BlockSpec((1, chunk, width), lambda b, c: (b, c, 0)),
                   pl.BlockSpec((1, chunk, width), lambda b, c: (b, nc - 1 - c, 0))] + state_specs,
        out_shape=[jax.ShapeDtypeStruct((bsz, seq, width), F32),
                   jax.ShapeDtypeStruct((bsz, seq, width), F32),
                   jax.ShapeDtypeStruct(s0.shape, F32),
                   jax.ShapeDtypeStruct(m0.shape, F32)],
        scratch_shapes=[pltpu.VMEM((nch, LANE, 2 * LANE), F32), pltpu.VMEM((nch, 8, LANE), F32)],
        compiler_params=_cparams("parallel", "arbitrary"),
        name="mlstm",
    )(y3, y3, kt, gc, gr, y3, y3, kt, gc, gr, s0, m0)


def _merge_body(ac_ref, gcur_ref, ap_ref, gp_ref, an_ref, gn_ref, o_ref, hf_ref, hb_ref,
                cw_ref, cb_ref, lg_ref, lb_ref, hg_ref, z_ref, scr, sh_scr, *, tm, tiles):
    i = pl.program_id(0)
    not_first = (i % tiles != 0).astype(F32)
    not_last = (i % tiles != tiles - 1).astype(F32)

    def glu(a, g):
        return a.astype(F32) * jax.nn.sigmoid(g.astype(F32))

    scr[0:HALO, :] = glu(ap_ref[...], gp_ref[...]) * not_first
    scr[HALO:HALO + tm, :] = glu(ac_ref[...], gcur_ref[...])
    scr[HALO + tm:2 * HALO + tm, :] = glu(an_ref[...], gn_ref[...]) * not_last
    rows = tm + 2 * HALO
    for r in range(SUBLANE):
        sh_scr[r, 0:rows - SUBLANE, :] = scr[r:rows - SUBLANE + r, :]
    first = HALO - (CONV_WIDTH - 1) // 2

    def row_block(rb, carry):
        base = pl.multiple_of(rb * CONV_ROWS, CONV_ROWS)
        acc = jnp.zeros((CONV_ROWS, CONV_DIM), F32)
        for k in range(CONV_WIDTH):
            off = first + k
            acc += sh_scr[off % SUBLANE, pl.ds(base + off - off % SUBLANE, CONV_ROWS), :] * cw_ref[k:k + 1, :]
        acc += cb_ref[...]
        mu = jnp.mean(acc, axis=-1, keepdims=True)
        cen = acc - mu
        var = jnp.mean(cen * cen, axis=-1, keepdims=True)
        u = cen * lax.rsqrt(var + EPS) * lg_ref[...] + lb_ref[...]
        z_ref[pl.ds(base, CONV_ROWS), :CONV_DIM] = (u * jax.nn.sigmoid(u)).astype(z_ref.dtype)
        return carry

    lax.fori_loop(0, tm // CONV_ROWS, row_block, 0)
    hs = hf_ref[...] + hb_ref[...]
    og = hg_ref[...] * jax.nn.sigmoid(o_ref[...].astype(F32))
    for hd in range(MLSTM_HEADS):
        sl = slice(hd * MLSTM_V, (hd + 1) * MLSTM_V)
        z_ref[:, CONV_DIM + hd * MLSTM_V:CONV_DIM + (hd + 1) * MLSTM_V] = (
            _rms_rows(hs[:, sl]) * og[:, sl]).astype(z_ref.dtype)


def _merge(y, hf, hb, conv_w, conv_b, ln_g, ln_b, head_gain, *, seq, tm, a_blk, g_blk, o_blk):
    m = y.shape[0]
    assert tm % CONV_ROWS == 0
    tiles = seq // tm
    r = tm // HALO
    last = m // HALO - 1
    cur = lambda blk: pl.BlockSpec((tm, CONV_DIM), lambda i: (i, blk))
    prev = lambda blk: pl.BlockSpec((HALO, CONV_DIM), lambda i: (jnp.maximum(i * r - 1, 0), blk))
    nxt = lambda blk: pl.BlockSpec((HALO, CONV_DIM), lambda i: (jnp.minimum((i + 1) * r, last), blk))
    vec = lambda a: a.reshape(1, -1)
    hspec = pl.BlockSpec((tm, CONV_DIM), lambda i: (i, 0))
    return pl.pallas_call(
        functools.partial(_merge_body, tm=tm, tiles=tiles),
        grid=(m // tm,),
        in_specs=[cur(a_blk), cur(g_blk), prev(a_blk), prev(g_blk), nxt(a_blk), nxt(g_blk), cur(o_blk),
                  hspec, hspec,
                  _const_spec((CONV_WIDTH, CONV_DIM)), _const_spec((1, CONV_DIM)), _const_spec((1, CONV_DIM)),
                  _const_spec((1, CONV_DIM)), _const_spec((1, CONV_DIM))],
        out_specs=pl.BlockSpec((tm, 2 * CONV_DIM), lambda i: (i, 0)),
        out_shape=jax.ShapeDtypeStruct((m, 2 * CONV_DIM), BF16),
        scratch_shapes=[pltpu.VMEM((tm + 2 * HALO, CONV_DIM), F32),
                        pltpu.VMEM((SUBLANE, tm + 2 * HALO, CONV_DIM), F32)],
        compiler_params=_cparams("parallel"),
        name="ab_merge",
    )(y, y, y, y, y, y, y, hf, hb, conv_w, vec(conv_b), vec(ln_g), vec(ln_b), vec(head_gain))


def _ffn_body(xc_ref, xp_ref, xn_ref, a_ref, b_ref, gate_ref, win_ref, cw_ref, cb_ref, wout_ref, o_ref,
              h_scr, u_scr, *, tm, tiles, f_chunk):
    i = pl.program_id(0)
    not_first = (i % tiles != 0).astype(F32)
    not_last = (i % tiles != tiles - 1).astype(F32)

    def modulate(x):
        return _rms_rows(x) * a_ref[0] + b_ref[0]

    h_scr[0:HALO, :] = (modulate(xp_ref[...]) * not_first).astype(BF16)
    h_scr[HALO:HALO + tm, :] = modulate(xc_ref[...]).astype(BF16)
    h_scr[HALO + tm:, :] = (modulate(xn_ref[...]) * not_last).astype(BF16)
    f = cw_ref.shape[1]
    for c0 in range(0, f, f_chunk):
        c1 = min(c0 + f_chunk, f)
        g = jnp.dot(h_scr[...], win_ref[:, c0:c1], preferred_element_type=F32)
        v = jnp.dot(h_scr[HALO:HALO + tm, :], win_ref[:, f + c0:f + c1], preferred_element_type=F32)
        y = (g[HALO - 1:HALO - 1 + tm] * cw_ref[0:1, c0:c1] + g[HALO:HALO + tm] * cw_ref[1:2, c0:c1]
             + g[HALO + 1:HALO + 1 + tm] * cw_ref[2:3, c0:c1] + cb_ref[:, c0:c1])
        u_scr[:, c0:c1] = (jax.nn.gelu(y, approximate=True) * v).astype(BF16)
    o_ref[...] = xc_ref[...] + gate_ref[0] * jnp.dot(u_scr[...], wout_ref[...], preferred_element_type=F32)


def _conv_ffn(x, mod_a, mod_b, gate, w_in, conv_w, conv_b, w_out, *, seq, tm):
    m, d = x.shape
    f = conv_w.shape[1]
    tiles = seq // tm
    r = tm // HALO
    last = m // HALO - 1
    grp = lambda i: (i // tiles, 0, 0)
    return pl.pallas_call(
        functools.partial(_ffn_body, tm=tm, tiles=tiles, f_chunk=256),
        grid=(m // tm,),
        in_specs=[pl.BlockSpec((tm, d), lambda i: (i, 0)),
                  pl.BlockSpec((HALO, d), lambda i: (jnp.maximum(i * r - 1, 0), 0)),
                  pl.BlockSpec((HALO, d), lambda i: (jnp.minimum((i + 1) * r, last), 0)),
                  pl.BlockSpec((1, 1, d), grp), pl.BlockSpec((1, 1, d), grp), pl.BlockSpec((1, 1, d), grp),
                  _const_spec(w_in.shape), _const_spec((3, f)), _const_spec((1, f)), _const_spec(w_out.shape)],
        out_specs=pl.BlockSpec((tm, d), lambda i: (i, 0)),
        out_shape=jax.ShapeDtypeStruct((m, d), F32),
        scratch_shapes=[pltpu.VMEM((tm + 2 * HALO, d), BF16), pltpu.VMEM((tm, f), BF16)],
        compiler_params=_cparams("parallel"),
        name="conv_ffn",
    )(x, x, x, mod_a, mod_b, gate, w_in, conv_w, conv_b.reshape(1, f), w_out)


def _head_rsqrt(xa, ind_ref, indt_ref):
    sq = xa * xa
    sq_hi = sq.astype(BF16)
    sq_lo = (sq - sq_hi.astype(F32)).astype(BF16)
    ms = (jnp.dot(sq_hi, ind_ref[...], preferred_element_type=F32)
          + jnp.dot(sq_lo, ind_ref[...], preferred_element_type=F32)) * (1.0 / MLA_QK)
    r_hi, r_lo, _ = _split3(lax.rsqrt(ms + EPS))
    return (jnp.dot(r_hi, indt_ref[...], preferred_element_type=F32)
            + jnp.dot(r_lo, indt_ref[...], preferred_element_type=F32))


def _q_prep_body(cq_ref, qn_ref, wa_ref, wb_ref, ind_ref, indt_ref, gc_ref, gs_ref, q_ref):
    h = (_rms_rows(cq_ref[...]) * qn_ref[...]).astype(BF16)
    xa = jnp.dot(h, wa_ref[...], preferred_element_type=F32)
    xb = jnp.dot(h, wb_ref[...], preferred_element_type=F32)
    rs = _head_rsqrt(xa, ind_ref, indt_ref)
    gc, gs = gc_ref[...], gs_ref[...]
    for hd in range(MLA_HEADS):
        sl = slice(hd * LANE, (hd + 1) * LANE)
        q_ref[0, hd] = (rs[:, sl] * (xa[:, sl] * gc + xb[:, sl] * gs)).astype(q_ref.dtype)


def _q_prep(ckv, q_norm, w_a, w_b, ind, indt, gcos, gsin, *, bsz, seq, tm):
    tiles = seq // tm
    tab = pl.BlockSpec((tm, LANE), lambda i: (i % tiles, 0))
    return pl.pallas_call(
        _q_prep_body,
        grid=(bsz * tiles,),
        in_specs=[pl.BlockSpec((tm, MLA_Q_LORA), lambda i: (i, 0)),
                  _const_spec((1, MLA_Q_LORA)), _const_spec(w_a.shape), _const_spec(w_b.shape),
                  _const_spec(ind.shape), _const_spec(indt.shape), tab, tab],
        out_specs=pl.BlockSpec((1, MLA_HEADS, tm, LANE), lambda i: (i // tiles, 0, i % tiles, 0)),
        out_shape=jax.ShapeDtypeStruct((bsz, MLA_HEADS, seq, LANE), BF16),
        compiler_params=_cparams("parallel"),
        name="mla_q_prep",
    )(ckv, q_norm.reshape(1, -1), w_a, w_b, ind, indt, gcos, gsin)


def _kv_prep_body(ckv_ref, kra_ref, krb_ref, kn_ref, wk_ref, wvt_ref, ind_ref, indt_ref, gc_ref, gs_ref,
                  k_ref, vt_ref):
    h = (_rms_rows(ckv_ref[...]) * kn_ref[...]).astype(BF16)
    gc = gc_ref[...]
    kra = kra_ref[...]
    xa = jnp.dot(h, wk_ref[...], preferred_element_type=F32)
    xa = xa + jnp.concatenate([kra] * MLA_HEADS, axis=1)
    rs = _head_rsqrt(xa, ind_ref, indt_ref)
    rot = krb_ref[...] * gs_ref[...]
    for hd in range(MLA_HEADS):
        sl = slice(hd * LANE, (hd + 1) * LANE)
        k_ref[0, hd] = (rs[:, sl] * (xa[:, sl] * gc + rot)).astype(k_ref.dtype)
    vt = lax.dot_general(wvt_ref[...], h, (((1,), (1,)), ((), ())), preferred_element_type=F32)
    tm = vt.shape[1]
    ones_rows = (lax.broadcasted_iota(jnp.int32, (VT_ROWS - MLA_V, tm), 0) == 0).astype(vt_ref.dtype)
    for hd in range(MLA_HEADS):
        vt_ref[0, hd, 0, :MLA_V, :] = vt[hd * MLA_V:(hd + 1) * MLA_V, :].astype(vt_ref.dtype)
        vt_ref[0, hd, 0, MLA_V:, :] = ones_rows


def _kv_prep(ckv, kv_norm, w_uk, w_uvt, ind, indt, gcos, gsin, *, bsz, seq, tm, vt_width):
    tiles = seq // tm
    sub = vt_width // tm
    ckv_blk = MLA_Q_LORA // MLA_KV_LORA
    kra_blk = (MLA_Q_LORA + MLA_KV_LORA) // LANE
    tab = pl.BlockSpec((tm, LANE), lambda i: (i % tiles, 0))
    return pl.pallas_call(
        _kv_prep_body,
        grid=(bsz * tiles,),
        in_specs=[pl.BlockSpec((tm, MLA_KV_LORA), lambda i: (i, ckv_blk)),
                  pl.BlockSpec((tm, LANE), lambda i: (i, kra_blk)),
                  pl.BlockSpec((tm, LANE), lambda i: (i, kra_blk + 1)),
                  _const_spec((1, MLA_KV_LORA)), _const_spec(w_uk.shape), _const_spec(w_uvt.shape),
                  _const_spec(ind.shape), _const_spec(indt.shape), tab, tab],
        out_specs=[pl.BlockSpec((1, MLA_HEADS, tm, LANE), lambda i: (i // tiles, 0, i % tiles, 0)),
                   pl.BlockSpec((1, MLA_HEADS, 1, VT_ROWS, tm),
                                lambda i: (i // tiles, 0, (i % tiles) // sub, 0, (i % tiles) % sub))],
        out_shape=[jax.ShapeDtypeStruct((bsz, MLA_HEADS, seq, LANE), BF16),
                   jax.ShapeDtypeStruct((bsz, MLA_HEADS, seq // vt_width, VT_ROWS, vt_width), BF16)],
        compiler_params=_cparams("parallel"),
        name="mla_kv_prep",
    )(ckv, ckv, ckv, kv_norm.reshape(1, -1), w_uk, w_uvt, ind, indt, gcos, gsin)


def _attn_body(q_ref, kc_ref, vtc_ref, kl_ref, vtl_ref, o_ref, s_a, s_b, p_a, p_b, acc_scr, m_scr, mb_scr,
               *, n_lat, heads, unroll):
    qs = [q_ref[0, hh] for hh in range(heads)]

    def scores(kt, hh):
        return lax.dot_general(kt, qs[hh], (((1,), (1,)), ((), ())), preferred_element_type=F32)

    for hh in range(heads):
        st = scores(kc_ref[0, hh], hh)
        m = jnp.max(st, axis=0, keepdims=True)
        m_scr[hh] = m
        acc_scr[hh] = jnp.dot(vtc_ref[0, hh, 0], jnp.exp2(st - m).astype(BF16), preferred_element_type=F32)
        st = scores(kl_ref[0, hh, 0], hh)
        s_a[hh] = st
        mb_scr[hh] = jnp.max(st, axis=0, keepdims=True)
        p_b[hh] = jnp.zeros(p_b.shape[1:], p_b.dtype)

    def block_step(j, s_cur, s_nxt, p_wr, p_rd):
        jn = jnp.minimum(j + 1, n_lat - 1)
        jp = jnp.maximum(j - 1, 0)
        for hh in range(heads):
            st_n = scores(kl_ref[0, hh, jn], hh)
            s_nxt[hh] = st_n
            pv = jnp.dot(vtl_ref[0, hh, jp], p_rd[hh], preferred_element_type=F32)
            m = m_scr[hh]
            m_new = jnp.maximum(m, mb_scr[hh])
            p_wr[hh] = jnp.exp2(s_cur[hh] - m_new).astype(BF16)
            acc_scr[hh] = jnp.exp2(m - m_new) * (acc_scr[hh] + pv)
            m_scr[hh] = m_new
            mb_scr[hh] = jnp.max(st_n, axis=0, keepdims=True)

    def trip(t, carry):
        for u in range(0, unroll, 2):
            block_step(unroll * t + u, s_a, s_b, p_a, p_b)
            block_step(unroll * t + u + 1, s_b, s_a, p_b, p_a)
        return carry

    lax.fori_loop(0, n_lat // unroll, trip, 0)
    outs = []
    for hh in range(heads):
        acc = acc_scr[hh] + jnp.dot(vtl_ref[0, hh, n_lat - 1], p_b[hh], preferred_element_type=F32)
        outs.append(acc[:MLA_V] / acc[MLA_V:MLA_V + 1])
    o_ref[0] = jnp.transpose(jnp.concatenate(outs, axis=0)).astype(o_ref.dtype)


def _attention(q, k_ctx, vt_ctx, k_lat, vt_lat, *, tq):
    bsz, nh, seq, _ = q.shape
    heads = LANE // MLA_V
    n_c = k_ctx.shape[2]
    n_lat, _, kb = vt_lat.shape[2:]
    unroll = 4 if n_lat % 4 == 0 else 2
    assert n_lat % unroll == 0
    k_lat = k_lat.reshape(bsz, nh, n_lat, kb, LANE)
    return pl.pallas_call(
        functools.partial(_attn_body, n_lat=n_lat, heads=heads, unroll=unroll),
        grid=(bsz, nh // heads, seq // tq),
        in_specs=[pl.BlockSpec((1, heads, tq, LANE), lambda b, hp, i: (b, hp, i, 0)),
                  pl.BlockSpec((1, heads, n_c, LANE), lambda b, hp, i: (b, hp, 0, 0)),
                  pl.BlockSpec((1, heads, 1, VT_ROWS, n_c), lambda b, hp, i: (b, hp, 0, 0, 0)),
                  pl.BlockSpec((1, heads, n_lat, kb, LANE), lambda b, hp, i: (b, hp, 0, 0, 0)),
                  pl.BlockSpec((1, heads, n_lat, VT_ROWS, kb), lambda b, hp, i: (b, hp, 0, 0, 0))],
        out_specs=pl.BlockSpec((1, tq, LANE), lambda b, hp, i: (b, i, hp)),
        out_shape=jax.ShapeDtypeStruct((bsz, seq, nh * MLA_V), BF16),
        scratch_shapes=[pltpu.VMEM((heads, kb, tq), F32), pltpu.VMEM((heads, kb, tq), F32),
                        pltpu.VMEM((heads, kb, tq), BF16), pltpu.VMEM((heads, kb, tq), BF16),
                        pltpu.VMEM((heads, VT_ROWS, tq), F32), pltpu.VMEM((heads, 1, tq), F32),
                        pltpu.VMEM((heads, 1, tq), F32)],
        compiler_params=_cparams("parallel", "parallel", "arbitrary"),
        name="mla_attention",
    )(q, k_ctx, vt_ctx, k_lat, vt_lat)


def _pad_heads(w, heads, dim):
    k = w.shape[0]
    w = w.reshape(k, heads, dim)
    return jnp.pad(w, ((0, 0), (0, 0), (0, LANE - dim))).reshape(k, heads * LANE)


def _swap_pairs(a):
    s = a.shape
    return a.reshape(*s[:-1], s[-1] // 2, 2)[..., ::-1].reshape(s)


def _rope_lanes(a, swapped):
    rope = a[..., MLA_NOPE:]
    lo = jnp.zeros_like(a[..., :MLA_NOPE]) if swapped else a[..., :MLA_NOPE]
    mid = _swap_pairs(rope) if swapped else rope
    return jnp.concatenate([lo, mid, jnp.zeros_like(rope)], axis=-1)


def _rope_tables(rows, gain):
    row = jnp.repeat(jnp.arange(rows, dtype=F32), GRID_W)
    col = jnp.tile(jnp.arange(GRID_W, dtype=F32), rows)
    half = MLA_ROPE // 2
    inv_freq = ROPE_BASE ** (-jnp.arange(0, half, 2, dtype=F32) / half)
    ang = jnp.concatenate([row[:, None] * inv_freq, col[:, None] * inv_freq], axis=-1)
    cos = jnp.repeat(jnp.cos(ang), 2, axis=-1)
    sin = jnp.repeat(jnp.sin(ang), 2, axis=-1) * jnp.tile(jnp.array([-1.0, 1.0], F32), MLA_ROPE // 2)
    n = rows * GRID_W
    cos_tab = jnp.concatenate([jnp.ones((n, MLA_NOPE), F32), cos, jnp.zeros((n, MLA_ROPE), F32)], axis=-1)
    sin_tab = jnp.concatenate([jnp.zeros((n, MLA_NOPE), F32), sin, jnp.zeros((n, MLA_ROPE), F32)], axis=-1)
    return cos_tab * _rope_lanes(gain, False), sin_tab * _rope_lanes(gain, True)


def _no_rope_tables(n, gain):
    return jnp.broadcast_to(_rope_lanes(gain, False), (n, LANE)), jnp.zeros((n, LANE), F32)


def _row_tile(seq, want):
    t = min(want, seq)
    assert seq % t == 0 and t % HALO == 0
    return t


def kernel(x, c, ctx, c_ctx, ada_w, ada_b, ab_w_in, ab_gate_bias, ab_conv_w, ab_conv_b, ab_ln_g, ab_ln_b,
           ab_head_gain, ab_w_out, mla_w_in, mla_q_norm, mla_kv_norm, mla_w_uq, mla_w_ukv, mla_q_gain,
           mla_k_gain, mla_w_out, ffn_w_in, ffn_conv_w, ffn_conv_b, ffn_w_out):
    bsz, seq, d = x.shape
    n_ctx = ctx.shape[1]
    assert bsz <= 7 and seq % GRID_W == 0
    tm_l = _row_tile(seq, 512)
    tm_c = _row_tile(n_ctx, 512)
    chunk_l = _row_tile(seq, 256)
    chunk_c = _row_tile(n_ctx, 256)

    c8 = jnp.zeros((8, d), F32).at[:bsz].set(c).at[bsz].set(c_ctx)
    mods = _ada_mods(c8, ada_w, ada_b)

    def mod_vecs(layer):
        parts = jnp.split(mods[layer], 6, axis=-1)
        lat = [p[:bsz, None, :] for p in parts]
        cx = [jnp.broadcast_to(p[bsz][None, None, :], (bsz, 1, d)) for p in parts]
        return lat, cx

    xl = x.reshape(bsz * seq, d)
    xc = ctx.reshape(bsz * n_ctx, d)

    def conv_ffn(xr, mod, layer, seq_len, tm):
        shift, scale, gate = mod[3], mod[4], mod[5]
        return _conv_ffn(xr, 1.0 + scale, shift, gate, ffn_w_in[layer].astype(BF16), ffn_conv_w[layer],
                         ffn_conv_b[layer], ffn_w_out[layer].astype(BF16), seq=seq_len, tm=tm)

    lat, cx = mod_vecs(0)
    w_in = ab_w_in[0]
    wa, wg, wq, wk, wv, wo, wgt = jnp.split(
        w_in, [512, 1024, 1280, 1536, 2048, 2560], axis=1)
    w_main = jnp.concatenate([wa, wg, wv, wo, _pad_heads(wq, MLSTM_HEADS, MLSTM_QK)], axis=1).astype(BF16)
    a_blk, g_blk, v_blk, o_blk, q_blk = 0, 1, 2, 3, 4
    w_kt = jnp.transpose(_pad_heads(wk * (MLSTM_QK ** -0.5), MLSTM_HEADS, MLSTM_QK)).astype(BF16)
    wgi = jnp.concatenate([wgt[:, 0:4], wgt[:, 8:12]], axis=1)
    wgf = jnp.concatenate([wgt[:, 4:8], wgt[:, 12:16]], axis=1)
    pad8 = lambda a: jnp.pad(a, ((0, 0), (0, LANE - 8)))
    wgate = jnp.concatenate([pad8(wgi), pad8(wgf)], axis=1)
    wgate_hi = wgate.astype(BF16)
    wgate_lo = (wgate - wgate_hi.astype(F32)).astype(BF16)
    w_gate = jnp.stack([wgate_hi, wgate_lo])
    gb = ab_gate_bias[0]
    gbias = jnp.concatenate([pad8(jnp.concatenate([gb[0:4], gb[8:12]])[None, :]),
                             pad8(jnp.concatenate([gb[4:8], gb[12:16]])[None, :])], axis=1)
    w_out0 = ab_w_out[0].astype(BF16)

    def mixer0(xr, mod, seq_len, tm, chunk, s0, m0):
        y, kt, gates = _ab_in(xr, 1.0 + mod[1], mod[0], w_main, w_kt, w_gate, bsz=bsz, seq=seq_len, tm=tm)
        gc, gr = _gate_prep(gates, gbias, bsz=bsz, seq=seq_len, chunk=chunk)
        hf, hb, s1, m1 = _mlstm(y, kt, gc, gr, s0, m0, bsz=bsz, seq=seq_len, chunk=chunk,
                                q_blk=q_blk, v_blk=v_blk)
        z = _merge(y, hf.reshape(-1, hf.shape[-1]), hb.reshape(-1, hb.shape[-1]), ab_conv_w[0], ab_conv_b[0],
                   ab_ln_g[0], ab_ln_b[0], ab_head_gain[0], seq=seq_len, tm=tm,
                   a_blk=a_blk, g_blk=g_blk, o_blk=o_blk)
        out = _mm(z, w_out0, tm=tm, out_dtype=F32, group_tiles=seq_len // tm, res=(xr, mod[2]), name="ab_out")
        return out, s1, m1

    nch = 2 * MLSTM_HEADS
    s_zero = jnp.zeros((bsz, nch, LANE, 2 * LANE), F32)
    m_zero = jnp.zeros((bsz, nch, 8, LANE), F32)
    xc, s_ctx, m_ctx = mixer0(xc, cx, n_ctx, tm_c, chunk_c, s_zero, m_zero)
    xl, _, _ = mixer0(xl, lat, seq, tm_l, chunk_l, s_ctx, m_ctx)
    xl = conv_ffn(xl, lat, 0, seq, tm_l)
    xc = conv_ffn(xc, cx, 0, n_ctx, tm_c)

    lat, cx = mod_vecs(1)
    w_in = mla_w_in[0]
    w_kr = w_in[:, MLA_Q_LORA + MLA_KV_LORA:]
    zpad = lambda n: jnp.zeros((d, n), F32)
    w_in1 = jnp.concatenate([w_in[:, :MLA_Q_LORA + MLA_KV_LORA],
                             zpad(MLA_NOPE), w_kr, zpad(MLA_ROPE),
                             zpad(MLA_NOPE), _swap_pairs(w_kr), zpad(MLA_ROPE)], axis=1).astype(BF16)
    w_uq = mla_w_uq[0].reshape(MLA_Q_LORA, MLA_HEADS, MLA_QK)
    w_uq_a = _rope_lanes(w_uq, False).reshape(MLA_Q_LORA, MLA_HEADS * LANE).astype(BF16)
    w_uq_b = _rope_lanes(w_uq, True).reshape(MLA_Q_LORA, MLA_HEADS * LANE).astype(BF16)
    w_ukv = mla_w_ukv[0].reshape(MLA_KV_LORA, MLA_HEADS, MLA_NOPE + MLA_V)
    w_uk = _pad_heads(w_ukv[..., :MLA_NOPE].reshape(MLA_KV_LORA, -1), MLA_HEADS, MLA_NOPE).astype(BF16)
    w_uvt = jnp.transpose(w_ukv[..., MLA_NOPE:].reshape(MLA_KV_LORA, -1)).astype(BF16)
    ind = (jnp.arange(MLA_HEADS * LANE)[:, None] // LANE == jnp.arange(LANE)[None, :]).astype(BF16)
    indt = jnp.transpose(ind)
    q_tabs = _rope_tables(seq // GRID_W, mla_q_gain[0] * (MLA_QK ** -0.5 * math.log2(math.e)))
    k_tabs = _rope_tables(seq // GRID_W, mla_k_gain[0])
    kc_tabs = _no_rope_tables(n_ctx, mla_k_gain[0])

    tk = 256
    kv_blk = min(512, seq // 2)
    assert seq % kv_blk == 0 and kv_blk % tk == 0 and n_ctx % tk == 0
    c_lat = _mm(xl, w_in1, tm=tm_l, out_dtype=F32, group_tiles=seq // tm_l,
                pro=(1.0 + lat[1], lat[0]), name="mla_in")
    c_ctx_ = _mm(xc, w_in1, tm=tm_c, out_dtype=F32, group_tiles=n_ctx // tm_c,
                 pro=(1.0 + cx[1], cx[0]), name="mla_in")
    q = _q_prep(c_lat, mla_q_norm[0], w_uq_a, w_uq_b, ind, indt, *q_tabs, bsz=bsz, seq=seq, tm=tk)
    k_lat, vt_lat = _kv_prep(c_lat, mla_kv_norm[0], w_uk, w_uvt, ind, indt, *k_tabs, bsz=bsz, seq=seq, tm=tk,
                             vt_width=kv_blk)
    k_ctx, vt_ctx = _kv_prep(c_ctx_, mla_kv_norm[0], w_uk, w_uvt, ind, indt, *kc_tabs, bsz=bsz, seq=n_ctx, tm=tk,
                             vt_width=n_ctx)
    att = _attention(q, k_ctx, vt_ctx, k_lat, vt_lat, tq=min(512, seq))
    xl = _mm(att.reshape(bsz * seq, -1), mla_w_out[0].astype(BF16), tm=tm_l, out_dtype=F32,
             group_tiles=seq // tm_l, res=(xl, lat[2]), name="mla_out")
    xl = conv_ffn(xl, lat, 1, seq, tm_l)
    return xl.reshape(bsz, seq, d)
```

```python
import functools
import math

import jax
import jax.numpy as jnp
from jax import lax
from jax.experimental import pallas as pl
from jax.experimental.pallas import tpu as pltpu

F32 = jnp.float32
BF16 = jnp.bfloat16

EPS = 1e-6
GRID_W = 64
CONV_DIM = 512
CONV_WIDTH = 31
MLSTM_HEADS = 4
MLSTM_QK = 64
MLSTM_V = 128
MLA_HEADS = 16
MLA_Q_LORA = 512
MLA_KV_LORA = 256
MLA_NOPE = 64
MLA_ROPE = 32
MLA_V = 64
MLA_QK = MLA_NOPE + MLA_ROPE
ROPE_BASE = 10000.0
FFN_DIM = 2816

LANE = 128
SUBLANE = 8
HALO = 16
CONV_ROWS = 64
MAX_EXP2_SPAN = 100.0
VT_ROWS = MLA_V + 16
VMEM_LIMIT = 48 * 1024 * 1024


def _cparams(*sem):
    return pltpu.CompilerParams(dimension_semantics=sem, vmem_limit_bytes=VMEM_LIMIT)


def _const_spec(shape):
    nd = len(shape)
    return pl.BlockSpec(shape, lambda *_: (0,) * nd, pipeline_mode=pl.Buffered(1))


def _split3(a):
    hi = a.astype(BF16)
    r = a - hi.astype(F32)
    mid = r.astype(BF16)
    lo = (r - mid.astype(F32)).astype(BF16)
    return hi, mid, lo


def _rms_rows(xf):
    return xf * lax.rsqrt(jnp.mean(xf * xf, axis=-1, keepdims=True) + EPS)


def _ada_body(c_ref, w_ref, b_ref, o_ref):
    c = c_ref[...]
    s = c * jax.nn.sigmoid(c)
    s_hi, s_lo, _ = _split3(s)
    w = w_ref[0]
    w_hi = w.astype(BF16)
    w_lo = (w - w_hi.astype(F32)).astype(BF16)
    acc = jnp.dot(s_hi, w_hi, preferred_element_type=F32)
    acc += jnp.dot(s_hi, w_lo, preferred_element_type=F32)
    acc += jnp.dot(s_lo, w_hi, preferred_element_type=F32)
    o_ref[0] = acc + b_ref[0]


def _ada_mods(c8, ada_w, ada_b):
    depth, d, n = ada_w.shape
    tn = 1536
    return pl.pallas_call(
        _ada_body,
        grid=(depth, n // tn),
        in_specs=[pl.BlockSpec((8, d), lambda l, j: (0, 0)),
                  pl.BlockSpec((1, d, tn), lambda l, j: (l, 0, j)),
                  pl.BlockSpec((1, 1, tn), lambda l, j: (l, 0, j))],
        out_specs=pl.BlockSpec((1, 8, tn), lambda l, j: (l, 0, j)),
        out_shape=jax.ShapeDtypeStruct((depth, 8, n), F32),
        compiler_params=_cparams("parallel", "parallel"),
        name="ada_mods",
    )(c8, ada_w, ada_b.reshape(depth, 1, n))


def _mm_body(*refs, has_pro, has_res, n_chunk):
    it = iter(refs)
    x_ref = next(it)
    if has_pro:
        a_ref, b_ref = next(it), next(it)
    w_ref = next(it)
    if has_res:
        r_ref, g_ref = next(it), next(it)
    o_ref = next(it)
    if has_pro:
        h = (_rms_rows(x_ref[...].astype(F32)) * a_ref[0] + b_ref[0]).astype(BF16)
    else:
        h = x_ref[...].astype(BF16)
    n = o_ref.shape[-1]
    for c0 in range(0, n, n_chunk):
        c1 = min(c0 + n_chunk, n)
        acc = jnp.dot(h, w_ref[:, c0:c1], preferred_element_type=F32)
        if has_res:
            acc = r_ref[:, c0:c1] + g_ref[0][:, c0:c1] * acc
        o_ref[:, c0:c1] = acc.astype(o_ref.dtype)


def _mm(x, w, *, tm, out_dtype, group_tiles, pro=None, res=None, x_cols=None, n_chunk=512, name):
    m = x.shape[0]
    k, n = w.shape
    xcol = 0 if x_cols is None else x_cols
    in_specs = [pl.BlockSpec((tm, k), lambda i: (i, xcol))]
    args = [x]
    if pro is not None:
        in_specs += [pl.BlockSpec((1, 1, k), lambda i: (i // group_tiles, 0, 0))] * 2
        args += list(pro)
    in_specs.append(_const_spec((k, n)))
    args.append(w)
    if res is not None:
        in_specs += [pl.BlockSpec((tm, n), lambda i: (i, 0)),
                     pl.BlockSpec((1, 1, n), lambda i: (i // group_tiles, 0, 0))]
        args += list(res)
    return pl.pallas_call(
        functools.partial(_mm_body, has_pro=pro is not None, has_res=res is not None, n_chunk=n_chunk),
        grid=(m // tm,),
        in_specs=in_specs,
        out_specs=pl.BlockSpec((tm, n), lambda i: (i, 0)),
        out_shape=jax.ShapeDtypeStruct((m, n), out_dtype),
        compiler_params=_cparams("parallel"),
        name=name,
    )(*args)


def _ab_in_body(x_ref, a_ref, b_ref, w_ref, wkt_ref, wg_ref, y_ref, kt_ref, g_ref, *, n_chunk):
    hf = _rms_rows(x_ref[...]) * a_ref[0] + b_ref[0]
    h = hf.astype(BF16)
    n = y_ref.shape[-1]
    for c0 in range(0, n, n_chunk):
        c1 = min(c0 + n_chunk, n)
        y_ref[:, c0:c1] = jnp.dot(h, w_ref[:, c0:c1], preferred_element_type=F32).astype(y_ref.dtype)
    kt_ref[0] = lax.dot_general(wkt_ref[...], h, (((1,), (1,)), ((), ())),
                                preferred_element_type=F32).astype(kt_ref.dtype)
    h_hi, h_lo, _ = _split3(hf)
    acc = jnp.dot(h_hi, wg_ref[0], preferred_element_type=F32)
    acc += jnp.dot(h_hi, wg_ref[1], preferred_element_type=F32)
    acc += jnp.dot(h_lo, wg_ref[0], preferred_element_type=F32)
    g_ref[...] = acc


def _ab_in(x, mod_a, mod_b, w_main, w_kt, w_gate, *, bsz, seq, tm):
    m, d = x.shape
    n = w_main.shape[1]
    tiles = seq // tm
    return pl.pallas_call(
        functools.partial(_ab_in_body, n_chunk=512),
        grid=(m // tm,),
        in_specs=[pl.BlockSpec((tm, d), lambda i: (i, 0)),
                  pl.BlockSpec((1, 1, d), lambda i: (i // tiles, 0, 0)),
                  pl.BlockSpec((1, 1, d), lambda i: (i // tiles, 0, 0)),
                  _const_spec(w_main.shape), _const_spec(w_kt.shape), _const_spec(w_gate.shape)],
        out_specs=[pl.BlockSpec((tm, n), lambda i: (i, 0)),
                   pl.BlockSpec((1, w_kt.shape[0], tm), lambda i: (i // tiles, 0, i % tiles)),
                   pl.BlockSpec((tm, 2 * LANE), lambda i: (i, 0))],
        out_shape=[jax.ShapeDtypeStruct((m, n), BF16),
                   jax.ShapeDtypeStruct((bsz, w_kt.shape[0], seq), BF16),
                   jax.ShapeDtypeStruct((m, 2 * LANE), F32)],
        compiler_params=_cparams("parallel"),
        name="ab_in",
    )(x, mod_a, mod_b, w_main, w_kt, w_gate)


def _gate_prep_body(g_ref, bias_ref, gc_ref, gr_ref, *, chunk):
    g = g_ref[0]
    li = g[:, :LANE] + bias_ref[:, :LANE]
    lf = jax.nn.log_sigmoid(g[:, LANE:] + bias_ref[:, LANE:])
    row = lax.broadcasted_iota(jnp.int32, (chunk, chunk), 0)
    col = lax.broadcasted_iota(jnp.int32, (chunk, chunk), 1)
    lower = (col <= row).astype(BF16)
    upper = (col >= row).astype(BF16)
    b_f = jnp.zeros((chunk, LANE), F32)
    b_b = jnp.zeros((chunk, LANE), F32)
    for piece in _split3(lf):
        b_f += jnp.dot(lower, piece, preferred_element_type=F32)
        b_b += jnp.dot(upper, piece, preferred_element_type=F32)
    lane = lax.broadcasted_iota(jnp.int32, (chunk, LANE), 1)
    fwd = lane < MLSTM_HEADS
    b = jnp.where(fwd, b_f, b_b)
    r = li - b
    tok = lax.broadcasted_iota(jnp.int32, (chunk, LANE), 0)
    rm_f, rm_b = r, r
    step = 1
    while step < chunk:
        rm_f = jnp.maximum(rm_f, jnp.where(tok >= step, pltpu.roll(rm_f, step, axis=0), -jnp.inf))
        rm_b = jnp.maximum(rm_b, jnp.where(tok < chunk - step, pltpu.roll(rm_b, chunk - step, axis=0), -jnp.inf))
        step *= 2
    gc_ref[0, :, :LANE] = b
    gc_ref[0, :, LANE:] = jnp.where(fwd, rm_f, rm_b)
    gr_ref[0] = jnp.transpose(r)[:8, :]


def _gate_prep(gates, bias, *, bsz, seq, chunk):
    g3 = gates.reshape(bsz, seq, 2 * LANE)
    nc = seq // chunk
    return pl.pallas_call(
        functools.partial(_gate_prep_body, chunk=chunk),
        grid=(bsz, nc),
        in_specs=[pl.BlockSpec((1, chunk, 2 * LANE), lambda b, c: (b, c, 0)),
                  pl.BlockSpec((1, 2 * LANE), lambda b, c: (0, 0))],
        out_specs=[pl.BlockSpec((1, chunk, 2 * LANE), lambda b, c: (b, c, 0)),
                   pl.BlockSpec((1, 8, chunk), lambda b, c: (b, 0, c))],
        out_shape=[jax.ShapeDtypeStruct((bsz, seq, 2 * LANE), F32),
                   jax.ShapeDtypeStruct((bsz, 8, seq), F32)],
        compiler_params=_cparams("parallel", "parallel"),
        name="gate_prep",
    )(g3, bias)


def _mlstm_body(qf_ref, vf_ref, ktf_ref, gcf_ref, grf_ref,
                qb_ref, vb_ref, ktb_ref, gcb_ref, grb_ref,
                s0_ref, m0_ref,
                hf_ref, hb_ref, s1_ref, m1_ref,
                s_scr, m_scr, *, chunk):
    c = pl.program_id(1)
    nc = pl.num_programs(1)

    @pl.when(c == 0)
    def _():
        s_scr[...] = s0_ref[0]
        m_scr[...] = m0_ref[0]

    row = lax.broadcasted_iota(jnp.int32, (chunk, chunk), 0)
    col = lax.broadcasted_iota(jnp.int32, (chunk, chunk), 1)
    ones_col = (lax.broadcasted_iota(jnp.int32, (chunk, LANE), 1) == 0).astype(BF16)

    for d in range(2):
        q_ref, v_ref, kt_ref, gc_ref, gr_ref, h_ref = (
            (qf_ref, vf_ref, ktf_ref, gcf_ref, grf_ref, hf_ref) if d == 0 else
            (qb_ref, vb_ref, ktb_ref, gcb_ref, grb_ref, hb_ref))
        mask = (col <= row) if d == 0 else (col >= row)
        for hd in range(MLSTM_HEADS):
            ch = d * MLSTM_HEADS + hd
            sl = slice(hd * LANE, (hd + 1) * LANE)
            q = q_ref[0][:, sl]
            kt = kt_ref[0][sl, :]
            v = v_ref[0][:, sl]
            bcol = gc_ref[0][:, ch:ch + 1]
            rmcol = gc_ref[0][:, LANE + ch:LANE + ch + 1]
            rrow = gr_ref[0][ch:ch + 1, :]
            m_old = m_scr[ch][0:1, 0:1]
            b_last = bcol[chunk - 1:chunk, :] if d == 0 else bcol[0:1, :]
            cm = jnp.maximum(rmcol, m_old)
            s = jnp.dot(q, kt, preferred_element_type=F32) * jnp.exp(jnp.where(mask, rrow - cm, -jnp.inf))
            inter = jnp.exp(m_old - cm)
            v_ext = jnp.concatenate([v, ones_col], axis=1)
            st = s_scr[ch]
            nd = (jnp.dot(s.astype(BF16), v_ext, preferred_element_type=F32)
                  + inter * jnp.dot(q, st.astype(BF16), preferred_element_type=F32))
            num = nd[:, :LANE]
            den = nd[:, LANE:LANE + 1]
            h_ref[0, :, sl] = num * (1.0 / jnp.maximum(jnp.abs(den), jnp.exp(-(bcol + cm))))
            log_w = b_last + rrow
            m_new = jnp.maximum(b_last + m_old, jnp.max(log_w, axis=1, keepdims=True))
            w = jnp.exp(log_w - m_new)
            decay = jnp.exp(b_last + m_old - m_new)
            kw = (kt.astype(F32) * w).astype(BF16)
            s_scr[ch] = decay * st + jnp.dot(kw, v_ext, preferred_element_type=F32)
            m_scr[ch] = jnp.broadcast_to(m_new, (8, LANE))

    @pl.when(c == nc - 1)
    def _():
        s1_ref[0] = s_scr[...]
        m1_ref[0] = m_scr[...]


def _mlstm(y, kt, gc, gr, s0, m0, *, bsz, seq, chunk, q_blk, v_blk):
    nc = seq // chunk
    width = MLSTM_HEADS * LANE
    y3 = y.reshape(bsz, seq, y.shape[-1])
    nch = 2 * MLSTM_HEADS

    def fwd(b, c):
        return c

    def bwd(b, c):
        return nc - 1 - c

    def specs(pos):
        return [pl.BlockSpec((1, chunk, width), lambda b, c: (b, pos(b, c), q_blk)),
                pl.BlockSpec((1, chunk, width), lambda b, c: (b, pos(b, c), v_blk)),
                pl.BlockSpec((1, width, chunk), lambda b, c: (b, 0, pos(b, c))),
                pl.BlockSpec((1, chunk, 2 * LANE), lambda b, c: (b, pos(b, c), 0)),
                pl.BlockSpec((1, 8, chunk), lambda b, c: (b, 0, pos(b, c)))]

    state_specs = [pl.BlockSpec((1, nch, LANE, 2 * LANE), lambda b, c: (b, 0, 0, 0)),
                   pl.BlockSpec((1, nch, 8, LANE), lambda b, c: (b, 0, 0, 0))]
    return pl.pallas_call(
        functools.partial(_mlstm_body, chunk=chunk),
        grid=(bsz, nc),
        in_specs=specs(fwd) + specs(bwd) + state_specs,
        out_specs=[pl.BlockSpec((1, chunk, width), lambda b, c: (b, c, 0)),
                   pl.BlockSpec((1, chunk, width), lambda b, c: (b, nc - 1 - c, 0))] + state_specs,
        out_shape=[jax.ShapeDtypeStruct((bsz, seq, width), F32),
                   jax.ShapeDtypeStruct((bsz, seq, width), F32),
                   jax.ShapeDtypeStruct(s0.shape, F32),
                   jax.ShapeDtypeStruct(m0.shape, F32)],
        scratch_shapes=[pltpu.VMEM((nch, LANE, 2 * LANE), F32), pltpu.VMEM((nch, 8, LANE), F32)],
        compiler_params=_cparams("parallel", "arbitrary"),
        name="mlstm",
    )(y3, y3, kt, gc, gr, y3, y3, kt, gc, gr, s0, m0)


def _merge_body(ac_ref, gcur_ref, ap_ref, gp_ref, an_ref, gn_ref, o_ref, hf_ref, hb_ref,
                cw_ref, cb_ref, lg_ref, lb_ref, hg_ref, x_ref, gate_ref, wout_ref, out_ref,
                z_ref, scr, sh_scr, *, tm, tiles):
    i = pl.program_id(0)
    not_first = (i % tiles != 0).astype(F32)
    not_last = (i % tiles != tiles - 1).astype(F32)

    def glu(a, g):
        return a.astype(F32) * jax.nn.sigmoid(g.astype(F32))

    scr[0:HALO, :] = glu(ap_ref[...], gp_ref[...]) * not_first
    scr[HALO:HALO + tm, :] = glu(ac_ref[...], gcur_ref[...])
    scr[HALO + tm:2 * HALO + tm, :] = glu(an_ref[...], gn_ref[...]) * not_last
    rows = tm + 2 * HALO
    for r in range(SUBLANE):
        sh_scr[r, 0:rows - SUBLANE, :] = scr[r:rows - SUBLANE + r, :]
    first = HALO - (CONV_WIDTH - 1) // 2

    def row_block(rb, carry):
        base = pl.multiple_of(rb * CONV_ROWS, CONV_ROWS)
        acc = jnp.zeros((CONV_ROWS, CONV_DIM), F32)
        for k in range(CONV_WIDTH):
            off = first + k
            acc += sh_scr[off % SUBLANE, pl.ds(base + off - off % SUBLANE, CONV_ROWS), :] * cw_ref[k:k + 1, :]
        acc += cb_ref[...]
        mu = jnp.mean(acc, axis=-1, keepdims=True)
        cen = acc - mu
        var = jnp.mean(cen * cen, axis=-1, keepdims=True)
        u = cen * lax.rsqrt(var + EPS) * lg_ref[...] + lb_ref[...]
        z_ref[pl.ds(base, CONV_ROWS), :CONV_DIM] = (u * jax.nn.sigmoid(u)).astype(z_ref.dtype)
        return carry

    lax.fori_loop(0, tm // CONV_ROWS, row_block, 0)
    hs = hf_ref[...] + hb_ref[...]
    og = hg_ref[...] * jax.nn.sigmoid(o_ref[...].astype(F32))
    for hd in range(MLSTM_HEADS):
        sl = slice(hd * MLSTM_V, (hd + 1) * MLSTM_V)
        z_ref[:, CONV_DIM + hd * MLSTM_V:CONV_DIM + (hd + 1) * MLSTM_V] = (
            _rms_rows(hs[:, sl]) * og[:, sl]).astype(z_ref.dtype)
    out_ref[...] = x_ref[...] + gate_ref[0] * jnp.dot(z_ref[...], wout_ref[...], preferred_element_type=F32)


def _merge(y, hf, hb, conv_w, conv_b, ln_g, ln_b, head_gain, x, gate, w_out, *, seq, tm, a_blk, g_blk, o_blk):
    m = y.shape[0]
    d = x.shape[1]
    assert tm % CONV_ROWS == 0
    tiles = seq // tm
    r = tm // HALO
    last = m // HALO - 1
    cur = lambda blk: pl.BlockSpec((tm, CONV_DIM), lambda i: (i, blk))
    prev = lambda blk: pl.BlockSpec((HALO, CONV_DIM), lambda i: (jnp.maximum(i * r - 1, 0), blk))
    nxt = lambda blk: pl.BlockSpec((HALO, CONV_DIM), lambda i: (jnp.minimum((i + 1) * r, last), blk))
    vec = lambda a: a.reshape(1, -1)
    hspec = pl.BlockSpec((tm, CONV_DIM), lambda i: (i, 0))
    return pl.pallas_call(
        functools.partial(_merge_body, tm=tm, tiles=tiles),
        grid=(m // tm,),
        in_specs=[cur(a_blk), cur(g_blk), prev(a_blk), prev(g_blk), nxt(a_blk), nxt(g_blk), cur(o_blk),
                  hspec, hspec,
                  _const_spec((CONV_WIDTH, CONV_DIM)), _const_spec((1, CONV_DIM)), _const_spec((1, CONV_DIM)),
                  _const_spec((1, CONV_DIM)), _const_spec((1, CONV_DIM)),
                  pl.BlockSpec((tm, d), lambda i: (i, 0)), pl.BlockSpec((1, 1, d), lambda i: (i // tiles, 0, 0)),
                  _const_spec(w_out.shape)],
        out_specs=pl.BlockSpec((tm, d), lambda i: (i, 0)),
        out_shape=jax.ShapeDtypeStruct((m, d), F32),
        scratch_shapes=[pltpu.VMEM((tm, 2 * CONV_DIM), BF16),
                        pltpu.VMEM((tm + 2 * HALO, CONV_DIM), F32),
                        pltpu.VMEM((SUBLANE, tm + 2 * HALO, CONV_DIM), F32)],
        compiler_params=_cparams("parallel"),
        name="ab_merge_out",
    )(y, y, y, y, y, y, y, hf, hb, conv_w, vec(conv_b), vec(ln_g), vec(ln_b), vec(head_gain), x, gate, w_out)


def _ffn_body(xc_ref, xp_ref, xn_ref, a_ref, b_ref, gate_ref, win_ref, cw_ref, cb_ref, wout_ref, o_ref,
              h_scr, u_scr, *, tm, tiles, f_chunk):
    i = pl.program_id(0)
    not_first = (i % tiles != 0).astype(F32)
    not_last = (i % tiles != tiles - 1).astype(F32)

    def modulate(x):
        return _rms_rows(x) * a_ref[0] + b_ref[0]

    h_scr[0:HALO, :] = (modulate(xp_ref[...]) * not_first).astype(BF16)
    h_scr[HALO:HALO + tm, :] = modulate(xc_ref[...]).astype(BF16)
    h_scr[HALO + tm:, :] = (modulate(xn_ref[...]) * not_last).astype(BF16)
    f = cw_ref.shape[1]
    for c0 in range(0, f, f_chunk):
        c1 = min(c0 + f_chunk, f)
        g = jnp.dot(h_scr[...], win_ref[:, c0:c1], preferred_element_type=F32)
        v = jnp.dot(h_scr[HALO:HALO + tm, :], win_ref[:, f + c0:f + c1], preferred_element_type=F32)
        y = (g[HALO - 1:HALO - 1 + tm] * cw_ref[0:1, c0:c1] + g[HALO:HALO + tm] * cw_ref[1:2, c0:c1]
             + g[HALO + 1:HALO + 1 + tm] * cw_ref[2:3, c0:c1] + cb_ref[:, c0:c1])
        u_scr[:, c0:c1] = (jax.nn.gelu(y, approximate=True) * v).astype(BF16)
    o_ref[...] = xc_ref[...] + gate_ref[0] * jnp.dot(u_scr[...], wout_ref[...], preferred_element_type=F32)


def _conv_ffn(x, mod_a, mod_b, gate, w_in, conv_w, conv_b, w_out, *, seq, tm):
    m, d = x.shape
    f = conv_w.shape[1]
    tiles = seq // tm
    r = tm // HALO
    last = m // HALO - 1
    grp = lambda i: (i // tiles, 0, 0)
    return pl.pallas_call(
        functools.partial(_ffn_body, tm=tm, tiles=tiles, f_chunk=256),
        grid=(m // tm,),
        in_specs=[pl.BlockSpec((tm, d), lambda i: (i, 0)),
                  pl.BlockSpec((HALO, d), lambda i: (jnp.maximum(i * r - 1, 0), 0)),
                  pl.BlockSpec((HALO, d), lambda i: (jnp.minimum((i + 1) * r, last), 0)),
                  pl.BlockSpec((1, 1, d), grp), pl.BlockSpec((1, 1, d), grp), pl.BlockSpec((1, 1, d), grp),
                  _const_spec(w_in.shape), _const_spec((3, f)), _const_spec((1, f)), _const_spec(w_out.shape)],
        out_specs=pl.BlockSpec((tm, d), lambda i: (i, 0)),
        out_shape=jax.ShapeDtypeStruct((m, d), F32),
        scratch_shapes=[pltpu.VMEM((tm + 2 * HALO, d), BF16), pltpu.VMEM((tm, f), BF16)],
        compiler_params=_cparams("parallel"),
        name="conv_ffn",
    )(x, x, x, mod_a, mod_b, gate, w_in, conv_w, conv_b.reshape(1, f), w_out)


def _head_rsqrt(xa, ind_ref, indt_ref):
    sq = xa * xa
    sq_hi = sq.astype(BF16)
    sq_lo = (sq - sq_hi.astype(F32)).astype(BF16)
    ms = (jnp.dot(sq_hi, ind_ref[...], preferred_element_type=F32)
          + jnp.dot(sq_lo, ind_ref[...], preferred_element_type=F32)) * (1.0 / MLA_QK)
    r_hi, r_lo, _ = _split3(lax.rsqrt(ms + EPS))
    return (jnp.dot(r_hi, indt_ref[...], preferred_element_type=F32)
            + jnp.dot(r_lo, indt_ref[...], preferred_element_type=F32))


def _q_heads(cq, qn_ref, wa_ref, wb_ref, ind_ref, indt_ref, gc_ref, gs_ref, q_ref):
    h = (_rms_rows(cq) * qn_ref[...]).astype(BF16)
    xa = jnp.dot(h, wa_ref[...], preferred_element_type=F32)
    xb = jnp.dot(h, wb_ref[...], preferred_element_type=F32)
    rs = _head_rsqrt(xa, ind_ref, indt_ref)
    gc, gs = gc_ref[...], gs_ref[...]
    for hd in range(MLA_HEADS):
        sl = slice(hd * LANE, (hd + 1) * LANE)
        q_ref[0, hd] = (rs[:, sl] * (xa[:, sl] * gc + xb[:, sl] * gs)).astype(q_ref.dtype)


def _kv_heads(ckv, kra, krb, kn_ref, wk_ref, wvt_ref, ind_ref, indt_ref, gc_ref, gs_ref, k_ref, vt_ref):
    h = (_rms_rows(ckv) * kn_ref[...]).astype(BF16)
    gc = gc_ref[...]
    xa = jnp.dot(h, wk_ref[...], preferred_element_type=F32)
    xa = xa + jnp.concatenate([kra] * MLA_HEADS, axis=1)
    rs = _head_rsqrt(xa, ind_ref, indt_ref)
    rot = krb * gs_ref[...]
    one96 = (lax.broadcasted_iota(jnp.int32, (1, LANE), 1) == MLA_QK).astype(F32)
    for hd in range(MLA_HEADS):
        sl = slice(hd * LANE, (hd + 1) * LANE)
        k_ref[0, hd] = (rs[:, sl] * (xa[:, sl] * gc + rot) + one96).astype(k_ref.dtype)
    vt = lax.dot_general(wvt_ref[...], h, (((1,), (1,)), ((), ())), preferred_element_type=F32)
    tm = vt.shape[1]
    ones_rows = (lax.broadcasted_iota(jnp.int32, (VT_ROWS - MLA_V, tm), 0) == 0).astype(vt_ref.dtype)
    for hd in range(MLA_HEADS):
        vt_ref[0, hd, 0, :MLA_V, :] = vt[hd * MLA_V:(hd + 1) * MLA_V, :].astype(vt_ref.dtype)
        vt_ref[0, hd, 0, MLA_V:, :] = ones_rows


def _mla_prep_body(x_ref, a_ref, b_ref, win_ref, qn_ref, kn_ref, wa_ref, wb_ref, wk_ref, wvt_ref, ind_ref, indt_ref,
                   qgc_ref, qgs_ref, kgc_ref, kgs_ref, q_ref, k_ref, vt_ref):
    h = (_rms_rows(x_ref[...]) * a_ref[0] + b_ref[0]).astype(BF16)
    c = jnp.dot(h, win_ref[...], preferred_element_type=F32)
    kv0 = MLA_Q_LORA
    kr0 = MLA_Q_LORA + MLA_KV_LORA
    _q_heads(c[:, :kv0], qn_ref, wa_ref, wb_ref, ind_ref, indt_ref, qgc_ref, qgs_ref, q_ref)
    _kv_heads(c[:, kv0:kr0], c[:, kr0:kr0 + LANE], c[:, kr0 + LANE:], kn_ref, wk_ref, wvt_ref, ind_ref, indt_ref,
              kgc_ref, kgs_ref, k_ref, vt_ref)


def _mla_prep(x, mod_a, mod_b, w_in, q_norm, kv_norm, w_a, w_b, w_uk, w_uvt, ind, indt, q_tabs, k_tabs,
              *, bsz, seq, tm, vt_width):
    d = x.shape[1]
    tiles = seq // tm
    sub = vt_width // tm
    tab = pl.BlockSpec((tm, LANE), lambda i: (i % tiles, 0))
    grp = pl.BlockSpec((1, 1, d), lambda i: (i // tiles, 0, 0))
    head_major = pl.BlockSpec((1, MLA_HEADS, tm, LANE), lambda i: (i // tiles, 0, i % tiles, 0))
    return pl.pallas_call(
        _mla_prep_body,
        grid=(bsz * tiles,),
        in_specs=[pl.BlockSpec((tm, d), lambda i: (i, 0)), grp, grp, _const_spec(w_in.shape),
                  _const_spec((1, MLA_Q_LORA)), _const_spec((1, MLA_KV_LORA)),
                  _const_spec(w_a.shape), _const_spec(w_b.shape), _const_spec(w_uk.shape), _const_spec(w_uvt.shape),
                  _const_spec(ind.shape), _const_spec(indt.shape), tab, tab, tab, tab],
        out_specs=[head_major, head_major,
                   pl.BlockSpec((1, MLA_HEADS, 1, VT_ROWS, tm),
                                lambda i: (i // tiles, 0, (i % tiles) // sub, 0, (i % tiles) % sub))],
        out_shape=[jax.ShapeDtypeStruct((bsz, MLA_HEADS, seq, LANE), BF16),
                   jax.ShapeDtypeStruct((bsz, MLA_HEADS, seq, LANE), BF16),
                   jax.ShapeDtypeStruct((bsz, MLA_HEADS, seq // vt_width, VT_ROWS, vt_width), BF16)],
        compiler_params=_cparams("parallel"),
        name="mla_prep",
    )(x, mod_a, mod_b, w_in, q_norm.reshape(1, -1), kv_norm.reshape(1, -1), w_a, w_b, w_uk, w_uvt, ind, indt,
      *q_tabs, *k_tabs)


def _kv_prep_body(ckv_ref, kra_ref, krb_ref, kn_ref, wk_ref, wvt_ref, ind_ref, indt_ref, gc_ref, gs_ref,
                  k_ref, vt_ref):
    _kv_heads(ckv_ref[...], kra_ref[...], krb_ref[...], kn_ref, wk_ref, wvt_ref, ind_ref, indt_ref,
              gc_ref, gs_ref, k_ref, vt_ref)


def _kv_prep(ckv, kv_norm, w_uk, w_uvt, ind, indt, gcos, gsin, *, bsz, seq, tm, vt_width):
    tiles = seq // tm
    sub = vt_width // tm
    ckv_blk = MLA_Q_LORA // MLA_KV_LORA
    kra_blk = (MLA_Q_LORA + MLA_KV_LORA) // LANE
    tab = pl.BlockSpec((tm, LANE), lambda i: (i % tiles, 0))
    return pl.pallas_call(
        _kv_prep_body,
        grid=(bsz * tiles,),
        in_specs=[pl.BlockSpec((tm, MLA_KV_LORA), lambda i: (i, ckv_blk)),
                  pl.BlockSpec((tm, LANE), lambda i: (i, kra_blk)),
                  pl.BlockSpec((tm, LANE), lambda i: (i, kra_blk + 1)),
                  _const_spec((1, MLA_KV_LORA)), _const_spec(w_uk.shape), _const_spec(w_uvt.shape),
                  _const_spec(ind.shape), _const_spec(indt.shape), tab, tab],
        out_specs=[pl.BlockSpec((1, MLA_HEADS, tm, LANE), lambda i: (i // tiles, 0, i % tiles, 0)),
                   pl.BlockSpec((1, MLA_HEADS, 1, VT_ROWS, tm),
                                lambda i: (i // tiles, 0, (i % tiles) // sub, 0, (i % tiles) % sub))],
        out_shape=[jax.ShapeDtypeStruct((bsz, MLA_HEADS, seq, LANE), BF16),
                   jax.ShapeDtypeStruct((bsz, MLA_HEADS, seq // vt_width, VT_ROWS, vt_width), BF16)],
        compiler_params=_cparams("parallel"),
        name="mla_kv_prep",
    )(ckv, ckv, ckv, kv_norm.reshape(1, -1), w_uk, w_uvt, ind, indt, gcos, gsin)


def _attn_body(q_ref, kc_ref, vtc_ref, kl_ref, vtl_ref, o_ref, s_a, s_b, p_a, p_b, acc_scr, m_scr, mb_scr,
               *, n_lat, heads, unroll):
    qs = [q_ref[0, hh] for hh in range(heads)]

    def scores(kt, hh):
        return lax.dot_general(kt, qs[hh], (((1,), (1,)), ((), ())), preferred_element_type=F32)

    for hh in range(heads):
        st = scores(kc_ref[0, hh], hh)
        m = jnp.max(st, axis=0, keepdims=True)
        m_scr[hh] = m
        acc_scr[hh] = jnp.dot(vtc_ref[0, hh, 0], jnp.exp2(st - m).astype(BF16), preferred_element_type=F32)
        st = scores(kl_ref[0, hh, 0], hh)
        s_a[hh] = st
        mb_scr[hh] = jnp.max(st, axis=0, keepdims=True)
        p_b[hh] = jnp.zeros(p_b.shape[1:], p_b.dtype)

    def block_step(j, s_cur, s_nxt, p_wr, p_rd):
        jn = jnp.minimum(j + 1, n_lat - 1)
        jp = jnp.maximum(j - 1, 0)
        for hh in range(heads):
            st_n = scores(kl_ref[0, hh, jn], hh)
            s_nxt[hh] = st_n
            pv = jnp.dot(vtl_ref[0, hh, jp], p_rd[hh], preferred_element_type=F32)
            m = m_scr[hh]
            m_new = jnp.maximum(m, mb_scr[hh])
            p_wr[hh] = jnp.exp2(s_cur[hh] - m_new).astype(BF16)
            acc_scr[hh] = jnp.exp2(m - m_new) * (acc_scr[hh] + pv)
            m_scr[hh] = m_new
            mb_scr[hh] = jnp.max(st_n, axis=0, keepdims=True)

    def trip(t, carry):
        for u in range(0, unroll, 2):
            block_step(unroll * t + u, s_a, s_b, p_a, p_b)
            block_step(unroll * t + u + 1, s_b, s_a, p_b, p_a)
        return carry

    lax.fori_loop(0, n_lat // unroll, trip, 0)
    outs = []
    for hh in range(heads):
        acc = acc_scr[hh] + jnp.dot(vtl_ref[0, hh, n_lat - 1], p_b[hh], preferred_element_type=F32)
        outs.append(acc[:MLA_V] / acc[MLA_V:MLA_V + 1])
    o_ref[0] = jnp.transpose(jnp.concatenate(outs, axis=0)).astype(o_ref.dtype)


def _attention(q, k_ctx, vt_ctx, k_lat, vt_lat, *, tq):
    bsz, nh, seq, _ = q.shape
    heads = LANE // MLA_V
    n_c = k_ctx.shape[2]
    n_lat, _, kb = vt_lat.shape[2:]
    unroll = 4 if n_lat % 4 == 0 else 2
    assert n_lat % unroll == 0
    k_lat = k_lat.reshape(bsz, nh, n_lat, kb, LANE)
    return pl.pallas_call(
        functools.partial(_attn_body, n_lat=n_lat, heads=heads, unroll=unroll),
        grid=(bsz, nh // heads, seq // tq),
        in_specs=[pl.BlockSpec((1, heads, tq, LANE), lambda b, hp, i: (b, hp, i, 0)),
                  pl.BlockSpec((1, heads, n_c, LANE), lambda b, hp, i: (b, hp, 0, 0)),
                  pl.BlockSpec((1, heads, 1, VT_ROWS, n_c), lambda b, hp, i: (b, hp, 0, 0, 0)),
                  pl.BlockSpec((1, heads, n_lat, kb, LANE), lambda b, hp, i: (b, hp, 0, 0, 0)),
                  pl.BlockSpec((1, heads, n_lat, VT_ROWS, kb), lambda b, hp, i: (b, hp, 0, 0, 0))],
        out_specs=pl.BlockSpec((1, tq, LANE), lambda b, hp, i: (b, i, hp)),
        out_shape=jax.ShapeDtypeStruct((bsz, seq, nh * MLA_V), BF16),
        scratch_shapes=[pltpu.VMEM((heads, kb, tq), F32), pltpu.VMEM((heads, kb, tq), F32),
                        pltpu.VMEM((heads, kb, tq), BF16), pltpu.VMEM((heads, kb, tq), BF16),
                        pltpu.VMEM((heads, VT_ROWS, tq), F32), pltpu.VMEM((heads, 1, tq), F32),
                        pltpu.VMEM((heads, 1, tq), F32)],
        compiler_params=_cparams("parallel", "parallel", "arbitrary"),
        name="mla_attention",
    )(q, k_ctx, vt_ctx, k_lat, vt_lat)


def _attn_bounded_body(q_ref, kc_ref, vtc_ref, kl_ref, vtl_ref, shift_ref, o_ref, p_a, p_b, acc_scr,
                       *, n_lat, heads, unroll):
    shift = shift_ref[...].astype(BF16)
    qs = [q_ref[0, hh] + shift for hh in range(heads)]

    def probs(kt, hh):
        st = lax.dot_general(kt, qs[hh], (((1,), (1,)), ((), ())), preferred_element_type=F32)
        return jnp.exp2(st).astype(BF16)

    for hh in range(heads):
        acc_scr[hh] = jnp.dot(vtc_ref[0, hh, 0], probs(kc_ref[0, hh], hh), preferred_element_type=F32)
        p_a[hh] = probs(kl_ref[0, hh, 0], hh)

    def block_step(j, p_cur, p_nxt, produce):
        for hh in range(heads):
            if produce:
                p_nxt[hh] = probs(kl_ref[0, hh, j + 1], hh)
            acc_scr[hh] += jnp.dot(vtl_ref[0, hh, j], p_cur[hh], preferred_element_type=F32)

    def steps(j0, last):
        for u in range(0, unroll, 2):
            block_step(j0 + u, p_a, p_b, True)
            block_step(j0 + u + 1, p_b, p_a, not (last and u + 2 == unroll))

    def trip(t, carry):
        steps(unroll * t, False)
        return carry

    lax.fori_loop(0, n_lat // unroll - 1, trip, 0)
    steps(n_lat - unroll, True)
    outs = [acc_scr[hh][:MLA_V] / acc_scr[hh][MLA_V:MLA_V + 1] for hh in range(heads)]
    o_ref[0] = jnp.transpose(jnp.concatenate(outs, axis=0)).astype(o_ref.dtype)


def _attention_bounded(q, k_ctx, vt_ctx, k_lat, vt_lat, shift, *, tq):
    bsz, nh, seq, _ = q.shape
    heads = LANE // MLA_V
    n_c = k_ctx.shape[2]
    n_lat, _, kb = vt_lat.shape[2:]
    unroll = 4 if n_lat % 4 == 0 else 2
    assert n_lat % unroll == 0
    k_lat = k_lat.reshape(bsz, nh, n_lat, kb, LANE)
    return pl.pallas_call(
        functools.partial(_attn_bounded_body, n_lat=n_lat, heads=heads, unroll=unroll),
        grid=(bsz, nh // heads, seq // tq),
        in_specs=[pl.BlockSpec((1, heads, tq, LANE), lambda b, hp, i: (b, hp, i, 0)),
                  pl.BlockSpec((1, heads, n_c, LANE), lambda b, hp, i: (b, hp, 0, 0)),
                  pl.BlockSpec((1, heads, 1, VT_ROWS, n_c), lambda b, hp, i: (b, hp, 0, 0, 0)),
                  pl.BlockSpec((1, heads, n_lat, kb, LANE), lambda b, hp, i: (b, hp, 0, 0, 0)),
                  pl.BlockSpec((1, heads, n_lat, VT_ROWS, kb), lambda b, hp, i: (b, hp, 0, 0, 0)),
                  pl.BlockSpec((1, LANE), lambda b, hp, i: (0, 0))],
        out_specs=pl.BlockSpec((1, tq, LANE), lambda b, hp, i: (b, i, hp)),
        out_shape=jax.ShapeDtypeStruct((bsz, seq, nh * MLA_V), BF16),
        scratch_shapes=[pltpu.VMEM((heads, kb, tq), BF16), pltpu.VMEM((heads, kb, tq), BF16),
                        pltpu.VMEM((heads, VT_ROWS, tq), F32)],
        compiler_params=_cparams("parallel", "parallel", "arbitrary"),
        name="mla_attention_bounded",
    )(q, k_ctx, vt_ctx, k_lat, vt_lat, shift)


def _pad_heads(w, heads, dim):
    k = w.shape[0]
    w = w.reshape(k, heads, dim)
    return jnp.pad(w, ((0, 0), (0, 0), (0, LANE - dim))).reshape(k, heads * LANE)


def _swap_pairs(a):
    s = a.shape
    return a.reshape(*s[:-1], s[-1] // 2, 2)[..., ::-1].reshape(s)


def _rope_lanes(a, swapped):
    rope = a[..., MLA_NOPE:]
    lo = jnp.zeros_like(a[..., :MLA_NOPE]) if swapped else a[..., :MLA_NOPE]
    mid = _swap_pairs(rope) if swapped else rope
    return jnp.concatenate([lo, mid, jnp.zeros_like(rope)], axis=-1)


def _rope_tables(rows, gain):
    row = jnp.repeat(jnp.arange(rows, dtype=F32), GRID_W)
    col = jnp.tile(jnp.arange(GRID_W, dtype=F32), rows)
    half = MLA_ROPE // 2
    inv_freq = ROPE_BASE ** (-jnp.arange(0, half, 2, dtype=F32) / half)
    ang = jnp.concatenate([row[:, None] * inv_freq, col[:, None] * inv_freq], axis=-1)
    cos = jnp.repeat(jnp.cos(ang), 2, axis=-1)
    sin = jnp.repeat(jnp.sin(ang), 2, axis=-1) * jnp.tile(jnp.array([-1.0, 1.0], F32), MLA_ROPE // 2)
    n = rows * GRID_W
    cos_tab = jnp.concatenate([jnp.ones((n, MLA_NOPE), F32), cos, jnp.zeros((n, MLA_ROPE), F32)], axis=-1)
    sin_tab = jnp.concatenate([jnp.zeros((n, MLA_NOPE), F32), sin, jnp.zeros((n, MLA_ROPE), F32)], axis=-1)
    return cos_tab * _rope_lanes(gain, False), sin_tab * _rope_lanes(gain, True)


def _no_rope_tables(n, gain):
    return jnp.broadcast_to(_rope_lanes(gain, False), (n, LANE)), jnp.zeros((n, LANE), F32)


def _row_tile(seq, want):
    t = min(want, seq)
    assert seq % t == 0 and t % HALO == 0
    return t


def kernel(x, c, ctx, c_ctx, ada_w, ada_b, ab_w_in, ab_gate_bias, ab_conv_w, ab_conv_b, ab_ln_g, ab_ln_b,
           ab_head_gain, ab_w_out, mla_w_in, mla_q_norm, mla_kv_norm, mla_w_uq, mla_w_ukv, mla_q_gain,
           mla_k_gain, mla_w_out, ffn_w_in, ffn_conv_w, ffn_conv_b, ffn_w_out):
    bsz, seq, d = x.shape
    n_ctx = ctx.shape[1]
    assert bsz <= 7 and seq % GRID_W == 0
    tm_l = _row_tile(seq, 512)
    tm_c = _row_tile(n_ctx, 512)
    chunk_l = _row_tile(seq, 256)
    chunk_c = _row_tile(n_ctx, 256)

    c8 = jnp.zeros((8, d), F32).at[:bsz].set(c).at[bsz].set(c_ctx)
    mods = _ada_mods(c8, ada_w, ada_b)

    def mod_vecs(layer):
        parts = jnp.split(mods[layer], 6, axis=-1)
        lat = [p[:bsz, None, :] for p in parts]
        cx = [jnp.broadcast_to(p[bsz][None, None, :], (bsz, 1, d)) for p in parts]
        return lat, cx

    xl = x.reshape(bsz * seq, d)
    xc = ctx.reshape(bsz * n_ctx, d)

    def conv_ffn(xr, mod, layer, seq_len, tm):
        shift, scale, gate = mod[3], mod[4], mod[5]
        return _conv_ffn(xr, 1.0 + scale, shift, gate, ffn_w_in[layer].astype(BF16), ffn_conv_w[layer],
                         ffn_conv_b[layer], ffn_w_out[layer].astype(BF16), seq=seq_len, tm=tm)

    lat, cx = mod_vecs(0)
    w_in = ab_w_in[0]
    wa, wg, wq, wk, wv, wo, wgt = jnp.split(
        w_in, [512, 1024, 1280, 1536, 2048, 2560], axis=1)
    w_main = jnp.concatenate([wa, wg, wv, wo, _pad_heads(wq, MLSTM_HEADS, MLSTM_QK)], axis=1).astype(BF16)
    a_blk, g_blk, v_blk, o_blk, q_blk = 0, 1, 2, 3, 4
    w_kt = jnp.transpose(_pad_heads(wk * (MLSTM_QK ** -0.5), MLSTM_HEADS, MLSTM_QK)).astype(BF16)
    wgi = jnp.concatenate([wgt[:, 0:4], wgt[:, 8:12]], axis=1)
    wgf = jnp.concatenate([wgt[:, 4:8], wgt[:, 12:16]], axis=1)
    pad8 = lambda a: jnp.pad(a, ((0, 0), (0, LANE - 8)))
    wgate = jnp.concatenate([pad8(wgi), pad8(wgf)], axis=1)
    wgate_hi = wgate.astype(BF16)
    wgate_lo = (wgate - wgate_hi.astype(F32)).astype(BF16)
    w_gate = jnp.stack([wgate_hi, wgate_lo])
    gb = ab_gate_bias[0]
    gbias = jnp.concatenate([pad8(jnp.concatenate([gb[0:4], gb[8:12]])[None, :]),
                             pad8(jnp.concatenate([gb[4:8], gb[12:16]])[None, :])], axis=1)
    w_out0 = ab_w_out[0].astype(BF16)

    def mixer0(xr, mod, seq_len, tm, chunk, s0, m0):
        y, kt, gates = _ab_in(xr, 1.0 + mod[1], mod[0], w_main, w_kt, w_gate, bsz=bsz, seq=seq_len, tm=tm)
        gc, gr = _gate_prep(gates, gbias, bsz=bsz, seq=seq_len, chunk=chunk)
        hf, hb, s1, m1 = _mlstm(y, kt, gc, gr, s0, m0, bsz=bsz, seq=seq_len, chunk=chunk,
                                q_blk=q_blk, v_blk=v_blk)
        out = _merge(y, hf.reshape(-1, hf.shape[-1]), hb.reshape(-1, hb.shape[-1]), ab_conv_w[0], ab_conv_b[0],
                     ab_ln_g[0], ab_ln_b[0], ab_head_gain[0], xr, mod[2], w_out0, seq=seq_len, tm=tm,
                     a_blk=a_blk, g_blk=g_blk, o_blk=o_blk)
        return out, s1, m1

    nch = 2 * MLSTM_HEADS
    s_zero = jnp.zeros((bsz, nch, LANE, 2 * LANE), F32)
    m_zero = jnp.zeros((bsz, nch, 8, LANE), F32)
    xc, s_ctx, m_ctx = mixer0(xc, cx, n_ctx, tm_c, chunk_c, s_zero, m_zero)
    xl, _, _ = mixer0(xl, lat, seq, tm_l, chunk_l, s_ctx, m_ctx)
    xl = conv_ffn(xl, lat, 0, seq, tm_l)
    xc = conv_ffn(xc, cx, 0, n_ctx, tm_c)

    lat, cx = mod_vecs(1)
    w_in = mla_w_in[0]
    w_kr = w_in[:, MLA_Q_LORA + MLA_KV_LORA:]
    zpad = lambda n: jnp.zeros((d, n), F32)
    w_in1 = jnp.concatenate([w_in[:, :MLA_Q_LORA + MLA_KV_LORA],
                             zpad(MLA_NOPE), w_kr, zpad(MLA_ROPE),
                             zpad(MLA_NOPE), _swap_pairs(w_kr), zpad(MLA_ROPE)], axis=1).astype(BF16)
    w_uq = mla_w_uq[0].reshape(MLA_Q_LORA, MLA_HEADS, MLA_QK)
    w_uq_a = _rope_lanes(w_uq, False).reshape(MLA_Q_LORA, MLA_HEADS * LANE).astype(BF16)
    w_uq_b = _rope_lanes(w_uq, True).reshape(MLA_Q_LORA, MLA_HEADS * LANE).astype(BF16)
    w_ukv = mla_w_ukv[0].reshape(MLA_KV_LORA, MLA_HEADS, MLA_NOPE + MLA_V)
    w_uk = _pad_heads(w_ukv[..., :MLA_NOPE].reshape(MLA_KV_LORA, -1), MLA_HEADS, MLA_NOPE).astype(BF16)
    w_uvt = jnp.transpose(w_ukv[..., MLA_NOPE:].reshape(MLA_KV_LORA, -1)).astype(BF16)
    ind = (jnp.arange(MLA_HEADS * LANE)[:, None] // LANE == jnp.arange(LANE)[None, :]).astype(BF16)
    indt = jnp.transpose(ind)
    q_tabs = _rope_tables(seq // GRID_W, mla_q_gain[0] * (MLA_QK ** -0.5 * math.log2(math.e)))
    k_tabs = _rope_tables(seq // GRID_W, mla_k_gain[0])
    kc_tabs = _no_rope_tables(n_ctx, mla_k_gain[0])

    tk = 256
    kv_blk = min(512, seq // 2)
    assert seq % kv_blk == 0 and kv_blk % tk == 0 and n_ctx % tk == 0
    c_ctx_ = _mm(xc, w_in1, tm=tm_c, out_dtype=F32, group_tiles=n_ctx // tm_c,
                 pro=(1.0 + cx[1], cx[0]), name="mla_in")
    q, k_lat, vt_lat = _mla_prep(xl, 1.0 + lat[1], lat[0], w_in1, mla_q_norm[0], mla_kv_norm[0], w_uq_a, w_uq_b,
                                 w_uk, w_uvt, ind, indt, q_tabs, k_tabs, bsz=bsz, seq=seq, tm=tk, vt_width=kv_blk)
    k_ctx, vt_ctx = _kv_prep(c_ctx_, mla_kv_norm[0], w_uk, w_uvt, ind, indt, *kc_tabs, bsz=bsz, seq=n_ctx, tm=tk,
                             vt_width=n_ctx)
    bound = (1.02 * MLA_QK ** 0.5 * math.log2(math.e)) * jnp.max(jnp.abs(mla_q_gain[0])) * jnp.max(
        jnp.abs(mla_k_gain[0]))
    shift = jnp.zeros((1, LANE), F32).at[0, MLA_QK].set(-bound)
    tq = min(2048, seq)
    att = lax.cond(
        2.0 * bound < MAX_EXP2_SPAN,
        lambda: _attention_bounded(q, k_ctx, vt_ctx, k_lat, vt_lat, shift, tq=tq),
        lambda: _attention(q, k_ctx, vt_ctx, k_lat, vt_lat, tq=tq))
    xl = _mm(att.reshape(bsz * seq, -1), mla_w_out[0].astype(BF16), tm=tm_l, out_dtype=F32,
             group_tiles=seq // tm_l, res=(xl, lat[2]), name="mla_out")
    xl = conv_ffn(xl, lat, 1, seq, tm_l)
    return xl.reshape(bsz, seq, d)
```

```python
import functools
import math

import jax
import jax.numpy as jnp
from jax import lax
from jax.experimental import pallas as pl
from jax.experimental.pallas import tpu as pltpu

F32 = jnp.float32
BF16 = jnp.bfloat16

EPS = 1e-6
GRID_W = 64
CONV_DIM = 512
CONV_WIDTH = 31
MLSTM_HEADS = 4
MLSTM_QK = 64
MLSTM_V = 128
MLA_HEADS = 16
MLA_Q_LORA = 512
MLA_KV_LORA = 256
MLA_NOPE = 64
MLA_ROPE = 32
MLA_V = 64
MLA_QK = MLA_NOPE + MLA_ROPE
ROPE_BASE = 10000.0
FFN_DIM = 2816

LANE = 128
SUBLANE = 8
HALO = 16
CONV_ROWS = 64
MAX_EXP2_SPAN = 100.0
VT_ROWS = MLA_V + 16
VMEM_LIMIT = 48 * 1024 * 1024


def _cparams(*sem):
    return pltpu.CompilerParams(dimension_semantics=sem, vmem_limit_bytes=VMEM_LIMIT)


def _const_spec(shape):
    nd = len(shape)
    return pl.BlockSpec(shape, lambda *_: (0,) * nd, pipeline_mode=pl.Buffered(1))


def _split3(a):
    hi = a.astype(BF16)
    r = a - hi.astype(F32)
    mid = r.astype(BF16)
    lo = (r - mid.astype(F32)).astype(BF16)
    return hi, mid, lo


def _rms_rows(xf):
    return xf * lax.rsqrt(jnp.mean(xf * xf, axis=-1, keepdims=True) + EPS)


def _ada_body(c_ref, w_ref, b_ref, o_ref):
    c = c_ref[...]
    s = c * jax.nn.sigmoid(c)
    s_hi, s_lo, _ = _split3(s)
    w = w_ref[0]
    w_hi = w.astype(BF16)
    w_lo = (w - w_hi.astype(F32)).astype(BF16)
    acc = jnp.dot(s_hi, w_hi, preferred_element_type=F32)
    acc += jnp.dot(s_hi, w_lo, preferred_element_type=F32)
    acc += jnp.dot(s_lo, w_hi, preferred_element_type=F32)
    o_ref[0] = acc + b_ref[0]


def _ada_mods(c8, ada_w, ada_b):
    depth, d, n = ada_w.shape
    tn = 1536
    return pl.pallas_call(
        _ada_body,
        grid=(depth, n // tn),
        in_specs=[pl.BlockSpec((8, d), lambda l, j: (0, 0)),
                  pl.BlockSpec((1, d, tn), lambda l, j: (l, 0, j)),
                  pl.BlockSpec((1, 1, tn), lambda l, j: (l, 0, j))],
        out_specs=pl.BlockSpec((1, 8, tn), lambda l, j: (l, 0, j)),
        out_shape=jax.ShapeDtypeStruct((depth, 8, n), F32),
        compiler_params=_cparams("parallel", "parallel"),
        name="ada_mods",
    )(c8, ada_w, ada_b.reshape(depth, 1, n))


def _mm_body(*refs, has_pro, has_res, n_chunk):
    it = iter(refs)
    x_ref = next(it)
    if has_pro:
        a_ref, b_ref = next(it), next(it)
    w_ref = next(it)
    if has_res:
        r_ref, g_ref = next(it), next(it)
    o_ref = next(it)
    if has_pro:
        h = (_rms_rows(x_ref[...].astype(F32)) * a_ref[0] + b_ref[0]).astype(BF16)
    else:
        h = x_ref[...].astype(BF16)
    n = o_ref.shape[-1]
    for c0 in range(0, n, n_chunk):
        c1 = min(c0 + n_chunk, n)
        acc = jnp.dot(h, w_ref[:, c0:c1], preferred_element_type=F32)
        if has_res:
            acc = r_ref[:, c0:c1] + g_ref[0][:, c0:c1] * acc
        o_ref[:, c0:c1] = acc.astype(o_ref.dtype)


def _mm(x, w, *, tm, out_dtype, group_tiles, pro=None, res=None, x_cols=None, n_chunk=512, name):
    m = x.shape[0]
    k, n = w.shape
    xcol = 0 if x_cols is None else x_cols
    in_specs = [pl.BlockSpec((tm, k), lambda i: (i, xcol))]
    args = [x]
    if pro is not None:
        in_specs += [pl.BlockSpec((1, 1, k), lambda i: (i // group_tiles, 0, 0))] * 2
        args += list(pro)
    in_specs.append(_const_spec((k, n)))
    args.append(w)
    if res is not None:
        in_specs += [pl.BlockSpec((tm, n), lambda i: (i, 0)),
                     pl.BlockSpec((1, 1, n), lambda i: (i // group_tiles, 0, 0))]
        args += list(res)
    return pl.pallas_call(
        functools.partial(_mm_body, has_pro=pro is not None, has_res=res is not None, n_chunk=n_chunk),
        grid=(m // tm,),
        in_specs=in_specs,
        out_specs=pl.BlockSpec((tm, n), lambda i: (i, 0)),
        out_shape=jax.ShapeDtypeStruct((m, n), out_dtype),
        compiler_params=_cparams("parallel"),
        name=name,
    )(*args)


def _ab_in_body(x_ref, a_ref, b_ref, w_ref, wkt_ref, wg_ref, y_ref, kt_ref, g_ref, *, n_chunk):
    hf = _rms_rows(x_ref[...]) * a_ref[0] + b_ref[0]
    h = hf.astype(BF16)
    n = y_ref.shape[-1]
    for c0 in range(0, n, n_chunk):
        c1 = min(c0 + n_chunk, n)
        y_ref[:, c0:c1] = jnp.dot(h, w_ref[:, c0:c1], preferred_element_type=F32).astype(y_ref.dtype)
    kt_ref[0] = lax.dot_general(wkt_ref[...], h, (((1,), (1,)), ((), ())),
                                preferred_element_type=F32).astype(kt_ref.dtype)
    h_hi, h_lo, _ = _split3(hf)
    acc = jnp.dot(h_hi, wg_ref[0], preferred_element_type=F32)
    acc += jnp.dot(h_hi, wg_ref[1], preferred_element_type=F32)
    acc += jnp.dot(h_lo, wg_ref[0], preferred_element_type=F32)
    g_ref[...] = acc


def _ab_in(x, mod_a, mod_b, w_main, w_kt, w_gate, *, bsz, seq, tm):
    m, d = x.shape
    n = w_main.shape[1]
    tiles = seq // tm
    return pl.pallas_call(
        functools.partial(_ab_in_body, n_chunk=512),
        grid=(m // tm,),
        in_specs=[pl.BlockSpec((tm, d), lambda i: (i, 0)),
                  pl.BlockSpec((1, 1, d), lambda i: (i // tiles, 0, 0)),
                  pl.BlockSpec((1, 1, d), lambda i: (i // tiles, 0, 0)),
                  _const_spec(w_main.shape), _const_spec(w_kt.shape), _const_spec(w_gate.shape)],
        out_specs=[pl.BlockSpec((tm, n), lambda i: (i, 0)),
                   pl.BlockSpec((1, w_kt.shape[0], tm), lambda i: (i // tiles, 0, i % tiles)),
                   pl.BlockSpec((tm, 2 * LANE), lambda i: (i, 0))],
        out_shape=[jax.ShapeDtypeStruct((m, n), BF16),
                   jax.ShapeDtypeStruct((bsz, w_kt.shape[0], seq), BF16),
                   jax.ShapeDtypeStruct((m, 2 * LANE), F32)],
        compiler_params=_cparams("parallel"),
        name="ab_in",
    )(x, mod_a, mod_b, w_main, w_kt, w_gate)


def _gate_prep_body(g_ref, bias_ref, gc_ref, gr_ref, *, chunk):
    g = g_ref[0]
    li = g[:, :LANE] + bias_ref[:, :LANE]
    lf = jax.nn.log_sigmoid(g[:, LANE:] + bias_ref[:, LANE:])
    row = lax.broadcasted_iota(jnp.int32, (chunk, chunk), 0)
    col = lax.broadcasted_iota(jnp.int32, (chunk, chunk), 1)
    lower = (col <= row).astype(BF16)
    upper = (col >= row).astype(BF16)
    b_f = jnp.zeros((chunk, LANE), F32)
    b_b = jnp.zeros((chunk, LANE), F32)
    for piece in _split3(lf):
        b_f += jnp.dot(lower, piece, preferred_element_type=F32)
        b_b += jnp.dot(upper, piece, preferred_element_type=F32)
    lane = lax.broadcasted_iota(jnp.int32, (chunk, LANE), 1)
    fwd = lane < MLSTM_HEADS
    b = jnp.where(fwd, b_f, b_b)
    r = li - b
    tok = lax.broadcasted_iota(jnp.int32, (chunk, LANE), 0)
    rm_f, rm_b = r, r
    step = 1
    while step < chunk:
        rm_f = jnp.maximum(rm_f, jnp.where(tok >= step, pltpu.roll(rm_f, step, axis=0), -jnp.inf))
        rm_b = jnp.maximum(rm_b, jnp.where(tok < chunk - step, pltpu.roll(rm_b, chunk - step, axis=0), -jnp.inf))
        step *= 2
    gc_ref[0, :, :LANE] = b
    gc_ref[0, :, LANE:] = jnp.where(fwd, rm_f, rm_b)
    gr_ref[0] = jnp.transpose(r)[:8, :]


def _gate_prep(gates, bias, *, bsz, seq, chunk):
    g3 = gates.reshape(bsz, seq, 2 * LANE)
    nc = seq // chunk
    return pl.pallas_call(
        functools.partial(_gate_prep_body, chunk=chunk),
        grid=(bsz, nc),
        in_specs=[pl.BlockSpec((1, chunk, 2 * LANE), lambda b, c: (b, c, 0)),
                  pl.BlockSpec((1, 2 * LANE), lambda b, c: (0, 0))],
        out_specs=[pl.BlockSpec((1, chunk, 2 * LANE), lambda b, c: (b, c, 0)),
                   pl.BlockSpec((1, 8, chunk), lambda b, c: (b, 0, c))],
        out_shape=[jax.ShapeDtypeStruct((bsz, seq, 2 * LANE), F32),
                   jax.ShapeDtypeStruct((bsz, 8, seq), F32)],
        compiler_params=_cparams("parallel", "parallel"),
        name="gate_prep",
    )(g3, bias)


def _mlstm_body(qf_ref, vf_ref, ktf_ref, gcf_ref, grf_ref,
                qb_ref, vb_ref, ktb_ref, gcb_ref, grb_ref,
                s0_ref, m0_ref,
                hf_ref, hb_ref, s1_ref, m1_ref,
                s_scr, m_scr, *, chunk):
    c = pl.program_id(1)
    nc = pl.num_programs(1)

    @pl.when(c == 0)
    def _():
        s_scr[...] = s0_ref[0]
        m_scr[...] = m0_ref[0]

    row = lax.broadcasted_iota(jnp.int32, (chunk, chunk), 0)
    col = lax.broadcasted_iota(jnp.int32, (chunk, chunk), 1)
    ones_col = (lax.broadcasted_iota(jnp.int32, (chunk, LANE), 1) == 0).astype(BF16)

    for d in range(2):
        q_ref, v_ref, kt_ref, gc_ref, gr_ref, h_ref = (
            (qf_ref, vf_ref, ktf_ref, gcf_ref, grf_ref, hf_ref) if d == 0 else
            (qb_ref, vb_ref, ktb_ref, gcb_ref, grb_ref, hb_ref))
        mask = (col <= row) if d == 0 else (col >= row)
        for hd in range(MLSTM_HEADS):
            ch = d * MLSTM_HEADS + hd
            sl = slice(hd * LANE, (hd + 1) * LANE)
            q = q_ref[0][:, sl]
            kt = kt_ref[0][sl, :]
            v = v_ref[0][:, sl]
            bcol = gc_ref[0][:, ch:ch + 1]
            rmcol = gc_ref[0][:, LANE + ch:LANE + ch + 1]
            rrow = gr_ref[0][ch:ch + 1, :]
            m_old = m_scr[ch][0:1, 0:1]
            b_last = bcol[chunk - 1:chunk, :] if d == 0 else bcol[0:1, :]
            cm = jnp.maximum(rmcol, m_old)
            s = jnp.dot(q, kt, preferred_element_type=F32) * jnp.exp(jnp.where(mask, rrow - cm, -jnp.inf))
            inter = jnp.exp(m_old - cm)
            v_ext = jnp.concatenate([v, ones_col], axis=1)
            st = s_scr[ch]
            nd = (jnp.dot(s.astype(BF16), v_ext, preferred_element_type=F32)
                  + inter * jnp.dot(q, st.astype(BF16), preferred_element_type=F32))
            num = nd[:, :LANE]
            den = nd[:, LANE:LANE + 1]
            h_ref[0, :, sl] = num * (1.0 / jnp.maximum(jnp.abs(den), jnp.exp(-(bcol + cm))))
            log_w = b_last + rrow
            m_new = jnp.maximum(b_last + m_old, jnp.max(log_w, axis=1, keepdims=True))
            w = jnp.exp(log_w - m_new)
            decay = jnp.exp(b_last + m_old - m_new)
            kw = (kt.astype(F32) * w).astype(BF16)
            s_scr[ch] = decay * st + jnp.dot(kw, v_ext, preferred_element_type=F32)
            m_scr[ch] = jnp.broadcast_to(m_new, (8, LANE))

    @pl.when(c == nc - 1)
    def _():
        s1_ref[0] = s_scr[...]
        m1_ref[0] = m_scr[...]


def _mlstm(y, kt, gc, gr, s0, m0, *, bsz, seq, chunk, q_blk, v_blk):
    nc = seq // chunk
    width = MLSTM_HEADS * LANE
    y3 = y.reshape(bsz, seq, y.shape[-1])
    nch = 2 * MLSTM_HEADS

    def fwd(b, c):
        return c

    def bwd(b, c):
        return nc - 1 - c

    def specs(pos):
        return [pl.BlockSpec((1, chunk, width), lambda b, c: (b, pos(b, c), q_blk)),
                pl.BlockSpec((1, chunk, width), lambda b, c: (b, pos(b, c), v_blk)),
                pl.BlockSpec((1, width, chunk), lambda b, c: (b, 0, pos(b, c))),
                pl.BlockSpec((1, chunk, 2 * LANE), lambda b, c: (b, pos(b, c), 0)),
                pl.BlockSpec((1, 8, chunk), lambda b, c: (b, 0, pos(b, c)))]

    state_specs = [pl.BlockSpec((1, nch, LANE, 2 * LANE), lambda b, c: (b, 0, 0, 0)),
                   pl.BlockSpec((1, nch, 8, LANE), lambda b, c: (b, 0, 0, 0))]
    return pl.pallas_call(
        functools.partial(_mlstm_body, chunk=chunk),
        grid=(bsz, nc),
        in_specs=specs(fwd) + specs(bwd) + state_specs,
        out_specs=[pl.BlockSpec((1, chunk, width), lambda b, c: (b, c, 0)),
                   pl.BlockSpec((1, chunk, width), lambda b, c: (b, nc - 1 - c, 0))] + state_specs,
        out_shape=[jax.ShapeDtypeStruct((bsz, seq, width), F32),
                   jax.ShapeDtypeStruct((bsz, seq, width), F32),
                   jax.ShapeDtypeStruct(s0.shape, F32),
                   jax.ShapeDtypeStruct(m0.shape, F32)],
        scratch_shapes=[pltpu.VMEM((nch, LANE, 2 * LANE), F32), pltpu.VMEM((nch, 8, LANE), F32)],
        compiler_params=_cparams("parallel", "arbitrary"),
        name="mlstm",
    )(y3, y3, kt, gc, gr, y3, y3, kt, gc, gr, s0, m0)


def _merge_body(ac_ref, gcur_ref, ap_ref, gp_ref, an_ref, gn_ref, o_ref, hf_ref, hb_ref,
                cw_ref, cb_ref, lg_ref, lb_ref, hg_ref, x_ref, gate_ref, wout_ref, out_ref,
                z_ref, scr, sh_scr, *, tm, tiles):
    i = pl.program_id(0)
    not_first = (i % tiles != 0).astype(F32)
    not_last = (i % tiles != tiles - 1).astype(F32)

    def glu(a, g):
        return a.astype(F32) * jax.nn.sigmoid(g.astype(F32))

    scr[0:HALO, :] = glu(ap_ref[...], gp_ref[...]) * not_first
    scr[HALO:HALO + tm, :] = glu(ac_ref[...], gcur_ref[...])
    scr[HALO + tm:2 * HALO + tm, :] = glu(an_ref[...], gn_ref[...]) * not_last
    rows = tm + 2 * HALO
    for r in range(SUBLANE):
        sh_scr[r, 0:rows - SUBLANE, :] = scr[r:rows - SUBLANE + r, :]
    first = HALO - (CONV_WIDTH - 1) // 2

    def row_block(rb, carry):
        base = pl.multiple_of(rb * CONV_ROWS, CONV_ROWS)
        acc = jnp.zeros((CONV_ROWS, CONV_DIM), F32)
        for k in range(CONV_WIDTH):
            off = first + k
            acc += sh_scr[off % SUBLANE, pl.ds(base + off - off % SUBLANE, CONV_ROWS), :] * cw_ref[k:k + 1, :]
        acc += cb_ref[...]
        mu = jnp.mean(acc, axis=-1, keepdims=True)
        cen = acc - mu
        var = jnp.mean(cen * cen, axis=-1, keepdims=True)
        u = cen * lax.rsqrt(var + EPS) * lg_ref[...] + lb_ref[...]
        z_ref[pl.ds(base, CONV_ROWS), :CONV_DIM] = (u * jax.nn.sigmoid(u)).astype(z_ref.dtype)
        return carry

    lax.fori_loop(0, tm // CONV_ROWS, row_block, 0)
    hs = hf_ref[...] + hb_ref[...]
    og = hg_ref[...] * jax.nn.sigmoid(o_ref[...].astype(F32))
    for hd in range(MLSTM_HEADS):
        sl = slice(hd * MLSTM_V, (hd + 1) * MLSTM_V)
        z_ref[:, CONV_DIM + hd * MLSTM_V:CONV_DIM + (hd + 1) * MLSTM_V] = (
            _rms_rows(hs[:, sl]) * og[:, sl]).astype(z_ref.dtype)
    out_ref[...] = x_ref[...] + gate_ref[0] * jnp.dot(z_ref[...], wout_ref[...], preferred_element_type=F32)


def _merge(y, hf, hb, conv_w, conv_b, ln_g, ln_b, head_gain, x, gate, w_out, *, seq, tm, a_blk, g_blk, o_blk):
    m = y.shape[0]
    d = x.shape[1]
    assert tm % CONV_ROWS == 0
    tiles = seq // tm
    r = tm // HALO
    last = m // HALO - 1
    cur = lambda blk: pl.BlockSpec((tm, CONV_DIM), lambda i: (i, blk))
    prev = lambda blk: pl.BlockSpec((HALO, CONV_DIM), lambda i: (jnp.maximum(i * r - 1, 0), blk))
    nxt = lambda blk: pl.BlockSpec((HALO, CONV_DIM), lambda i: (jnp.minimum((i + 1) * r, last), blk))
    vec = lambda a: a.reshape(1, -1)
    hspec = pl.BlockSpec((tm, CONV_DIM), lambda i: (i, 0))
    return pl.pallas_call(
        functools.partial(_merge_body, tm=tm, tiles=tiles),
        grid=(m // tm,),
        in_specs=[cur(a_blk), cur(g_blk), prev(a_blk), prev(g_blk), nxt(a_blk), nxt(g_blk), cur(o_blk),
                  hspec, hspec,
                  _const_spec((CONV_WIDTH, CONV_DIM)), _const_spec((1, CONV_DIM)), _const_spec((1, CONV_DIM)),
                  _const_spec((1, CONV_DIM)), _const_spec((1, CONV_DIM)),
                  pl.BlockSpec((tm, d), lambda i: (i, 0)), pl.BlockSpec((1, 1, d), lambda i: (i // tiles, 0, 0)),
                  _const_spec(w_out.shape)],
        out_specs=pl.BlockSpec((tm, d), lambda i: (i, 0)),
        out_shape=jax.ShapeDtypeStruct((m, d), F32),
        scratch_shapes=[pltpu.VMEM((tm, 2 * CONV_DIM), BF16),
                        pltpu.VMEM((tm + 2 * HALO, CONV_DIM), F32),
                        pltpu.VMEM((SUBLANE, tm + 2 * HALO, CONV_DIM), F32)],
        compiler_params=_cparams("parallel"),
        name="ab_merge_out",
    )(y, y, y, y, y, y, y, hf, hb, conv_w, vec(conv_b), vec(ln_g), vec(ln_b), vec(head_gain), x, gate, w_out)


def _ffn_body(xc_ref, xp_ref, xn_ref, a_ref, b_ref, gate_ref, win_ref, cw_ref, cb_ref, wout_ref, o_ref,
              h_scr, u_scr, *, tm, tiles, f_chunk):
    i = pl.program_id(0)
    not_first = (i % tiles != 0).astype(F32)
    not_last = (i % tiles != tiles - 1).astype(F32)

    def modulate(x):
        return _rms_rows(x) * a_ref[0] + b_ref[0]

    h_scr[0:HALO, :] = (modulate(xp_ref[...]) * not_first).astype(BF16)
    h_scr[HALO:HALO + tm, :] = modulate(xc_ref[...]).astype(BF16)
    h_scr[HALO + tm:, :] = (modulate(xn_ref[...]) * not_last).astype(BF16)
    f = cw_ref.shape[1]
    for c0 in range(0, f, f_chunk):
        c1 = min(c0 + f_chunk, f)
        g = jnp.dot(h_scr[...], win_ref[:, c0:c1], preferred_element_type=F32)
        v = jnp.dot(h_scr[HALO:HALO + tm, :], win_ref[:, f + c0:f + c1], preferred_element_type=F32)
        y = (g[HALO - 1:HALO - 1 + tm] * cw_ref[0:1, c0:c1] + g[HALO:HALO + tm] * cw_ref[1:2, c0:c1]
             + g[HALO + 1:HALO + 1 + tm] * cw_ref[2:3, c0:c1] + cb_ref[:, c0:c1])
        u_scr[:, c0:c1] = (jax.nn.gelu(y, approximate=True) * v).astype(BF16)
    o_ref[...] = xc_ref[...] + gate_ref[0] * jnp.dot(u_scr[...], wout_ref[...], preferred_element_type=F32)


def _conv_ffn(x, mod_a, mod_b, gate, w_in, conv_w, conv_b, w_out, *, seq, tm):
    m, d = x.shape
    f = conv_w.shape[1]
    tiles = seq // tm
    r = tm // HALO
    last = m // HALO - 1
    grp = lambda i: (i // tiles, 0, 0)
    return pl.pallas_call(
        functools.partial(_ffn_body, tm=tm, tiles=tiles, f_chunk=256),
        grid=(m // tm,),
        in_specs=[pl.BlockSpec((tm, d), lambda i: (i, 0)),
                  pl.BlockSpec((HALO, d), lambda i: (jnp.maximum(i * r - 1, 0), 0)),
                  pl.BlockSpec((HALO, d), lambda i: (jnp.minimum((i + 1) * r, last), 0)),
                  pl.BlockSpec((1, 1, d), grp), pl.BlockSpec((1, 1, d), grp), pl.BlockSpec((1, 1, d), grp),
                  _const_spec(w_in.shape), _const_spec((3, f)), _const_spec((1, f)), _const_spec(w_out.shape)],
        out_specs=pl.BlockSpec((tm, d), lambda i: (i, 0)),
        out_shape=jax.ShapeDtypeStruct((m, d), F32),
        scratch_shapes=[pltpu.VMEM((tm + 2 * HALO, d), BF16), pltpu.VMEM((tm, f), BF16)],
        compiler_params=_cparams("parallel"),
        name="conv_ffn",
    )(x, x, x, mod_a, mod_b, gate, w_in, conv_w, conv_b.reshape(1, f), w_out)


def _head_rsqrt(xa, ind_ref):
    sq = xa * xa
    sq_hi = sq.astype(BF16)
    sq_lo = (sq - sq_hi.astype(F32)).astype(BF16)
    ms = (jnp.dot(sq_hi, ind_ref[...], preferred_element_type=F32)
          + jnp.dot(sq_lo, ind_ref[...], preferred_element_type=F32)) * (1.0 / MLA_QK)
    return lax.rsqrt(ms + EPS)


def _q_heads(cq, qn_ref, wa_ref, wb_ref, ind_ref, gc_ref, gs_ref, q_ref):
    h = (_rms_rows(cq) * qn_ref[...]).astype(BF16)
    xa = jnp.dot(h, wa_ref[...], preferred_element_type=F32)
    xb = jnp.dot(h, wb_ref[...], preferred_element_type=F32)
    rs = _head_rsqrt(xa, ind_ref)
    gc, gs = gc_ref[...], gs_ref[...]
    for hd in range(MLA_HEADS):
        sl = slice(hd * LANE, (hd + 1) * LANE)
        q_ref[0, hd] = (rs[:, hd:hd + 1] * (xa[:, sl] * gc + xb[:, sl] * gs)).astype(q_ref.dtype)


def _kv_heads(ckv, kra, krb, kn_ref, wk_ref, wvt_ref, ind_ref, gc_ref, gs_ref, k_ref, vt_ref):
    h = (_rms_rows(ckv) * kn_ref[...]).astype(BF16)
    gc = gc_ref[...]
    xa = jnp.dot(h, wk_ref[...], preferred_element_type=F32)
    xa = xa + jnp.concatenate([kra] * MLA_HEADS, axis=1)
    rs = _head_rsqrt(xa, ind_ref)
    rot = krb * gs_ref[...]
    one96 = (lax.broadcasted_iota(jnp.int32, (1, LANE), 1) == MLA_QK).astype(F32)
    for hd in range(MLA_HEADS):
        sl = slice(hd * LANE, (hd + 1) * LANE)
        k_ref[0, hd] = (rs[:, hd:hd + 1] * (xa[:, sl] * gc + rot) + one96).astype(k_ref.dtype)
    vt = lax.dot_general(wvt_ref[...], h, (((1,), (1,)), ((), ())), preferred_element_type=F32)
    tm = vt.shape[1]
    ones_rows = (lax.broadcasted_iota(jnp.int32, (VT_ROWS - MLA_V, tm), 0) == 0).astype(vt_ref.dtype)
    for hd in range(MLA_HEADS):
        vt_ref[0, hd, 0, :MLA_V, :] = vt[hd * MLA_V:(hd + 1) * MLA_V, :].astype(vt_ref.dtype)
        vt_ref[0, hd, 0, MLA_V:, :] = ones_rows


def _mla_prep_body(x_ref, a_ref, b_ref, win_ref, qn_ref, kn_ref, wa_ref, wb_ref, wk_ref, wvt_ref, ind_ref,
                   qgc_ref, qgs_ref, kgc_ref, kgs_ref, q_ref, k_ref, vt_ref):
    h = (_rms_rows(x_ref[...]) * a_ref[0] + b_ref[0]).astype(BF16)
    c = jnp.dot(h, win_ref[...], preferred_element_type=F32)
    kv0 = MLA_Q_LORA
    kr0 = MLA_Q_LORA + MLA_KV_LORA
    _q_heads(c[:, :kv0], qn_ref, wa_ref, wb_ref, ind_ref, qgc_ref, qgs_ref, q_ref)
    _kv_heads(c[:, kv0:kr0], c[:, kr0:kr0 + LANE], c[:, kr0 + LANE:], kn_ref, wk_ref, wvt_ref, ind_ref,
              kgc_ref, kgs_ref, k_ref, vt_ref)


def _mla_prep(x, mod_a, mod_b, w_in, q_norm, kv_norm, w_a, w_b, w_uk, w_uvt, ind, q_tabs, k_tabs,
              *, bsz, seq, tm, vt_width):
    d = x.shape[1]
    tiles = seq // tm
    sub = vt_width // tm
    tab = pl.BlockSpec((tm, LANE), lambda i: (i % tiles, 0))
    grp = pl.BlockSpec((1, 1, d), lambda i: (i // tiles, 0, 0))
    head_major = pl.BlockSpec((1, MLA_HEADS, tm, LANE), lambda i: (i // tiles, 0, i % tiles, 0))
    return pl.pallas_call(
        _mla_prep_body,
        grid=(bsz * tiles,),
        in_specs=[pl.BlockSpec((tm, d), lambda i: (i, 0)), grp, grp, _const_spec(w_in.shape),
                  _const_spec((1, MLA_Q_LORA)), _const_spec((1, MLA_KV_LORA)),
                  _const_spec(w_a.shape), _const_spec(w_b.shape), _const_spec(w_uk.shape), _const_spec(w_uvt.shape),
                  _const_spec(ind.shape), tab, tab, tab, tab],
        out_specs=[head_major, head_major,
                   pl.BlockSpec((1, MLA_HEADS, 1, VT_ROWS, tm),
                                lambda i: (i // tiles, 0, (i % tiles) // sub, 0, (i % tiles) % sub))],
        out_shape=[jax.ShapeDtypeStruct((bsz, MLA_HEADS, seq, LANE), BF16),
                   jax.ShapeDtypeStruct((bsz, MLA_HEADS, seq, LANE), BF16),
                   jax.ShapeDtypeStruct((bsz, MLA_HEADS, seq // vt_width, VT_ROWS, vt_width), BF16)],
        compiler_params=_cparams("parallel"),
        name="mla_prep",
    )(x, mod_a, mod_b, w_in, q_norm.reshape(1, -1), kv_norm.reshape(1, -1), w_a, w_b, w_uk, w_uvt, ind,
      *q_tabs, *k_tabs)


def _kv_prep_body(ckv_ref, kra_ref, krb_ref, kn_ref, wk_ref, wvt_ref, ind_ref, gc_ref, gs_ref,
                  k_ref, vt_ref):
    _kv_heads(ckv_ref[...], kra_ref[...], krb_ref[...], kn_ref, wk_ref, wvt_ref, ind_ref,
              gc_ref, gs_ref, k_ref, vt_ref)


def _kv_prep(ckv, kv_norm, w_uk, w_uvt, ind, gcos, gsin, *, bsz, seq, tm, vt_width):
    tiles = seq // tm
    sub = vt_width // tm
    ckv_blk = MLA_Q_LORA // MLA_KV_LORA
    kra_blk = (MLA_Q_LORA + MLA_KV_LORA) // LANE
    tab = pl.BlockSpec((tm, LANE), lambda i: (i % tiles, 0))
    return pl.pallas_call(
        _kv_prep_body,
        grid=(bsz * tiles,),
        in_specs=[pl.BlockSpec((tm, MLA_KV_LORA), lambda i: (i, ckv_blk)),
                  pl.BlockSpec((tm, LANE), lambda i: (i, kra_blk)),
                  pl.BlockSpec((tm, LANE), lambda i: (i, kra_blk + 1)),
                  _const_spec((1, MLA_KV_LORA)), _const_spec(w_uk.shape), _const_spec(w_uvt.shape),
                  _const_spec(ind.shape), tab, tab],
        out_specs=[pl.BlockSpec((1, MLA_HEADS, tm, LANE), lambda i: (i // tiles, 0, i % tiles, 0)),
                   pl.BlockSpec((1, MLA_HEADS, 1, VT_ROWS, tm),
                                lambda i: (i // tiles, 0, (i % tiles) // sub, 0, (i % tiles) % sub))],
        out_shape=[jax.ShapeDtypeStruct((bsz, MLA_HEADS, seq, LANE), BF16),
                   jax.ShapeDtypeStruct((bsz, MLA_HEADS, seq // vt_width, VT_ROWS, vt_width), BF16)],
        compiler_params=_cparams("parallel"),
        name="mla_kv_prep",
    )(ckv, ckv, ckv, kv_norm.reshape(1, -1), w_uk, w_uvt, ind, gcos, gsin)


def _attn_body(q_ref, kc_ref, vtc_ref, kl_ref, vtl_ref, o_ref, s_a, s_b, p_a, p_b, acc_scr, m_scr, mb_scr,
               *, n_lat, heads, unroll):
    qs = [q_ref[0, hh] for hh in range(heads)]

    def scores(kt, hh):
        return lax.dot_general(kt, qs[hh], (((1,), (1,)), ((), ())), preferred_element_type=F32)

    for hh in range(heads):
        st = scores(kc_ref[0, hh], hh)
        m = jnp.max(st, axis=0, keepdims=True)
        m_scr[hh] = m
        acc_scr[hh] = jnp.dot(vtc_ref[0, hh, 0], jnp.exp2(st - m).astype(BF16), preferred_element_type=F32)
        st = scores(kl_ref[0, hh, 0], hh)
        s_a[hh] = st
        mb_scr[hh] = jnp.max(st, axis=0, keepdims=True)
        p_b[hh] = jnp.zeros(p_b.shape[1:], p_b.dtype)

    def block_step(j, s_cur, s_nxt, p_wr, p_rd):
        jn = jnp.minimum(j + 1, n_lat - 1)
        jp = jnp.maximum(j - 1, 0)
        for hh in range(heads):
            st_n = scores(kl_ref[0, hh, jn], hh)
            s_nxt[hh] = st_n
            pv = jnp.dot(vtl_ref[0, hh, jp], p_rd[hh], preferred_element_type=F32)
            m = m_scr[hh]
            m_new = jnp.maximum(m, mb_scr[hh])
            p_wr[hh] = jnp.exp2(s_cur[hh] - m_new).astype(BF16)
            acc_scr[hh] = jnp.exp2(m - m_new) * (acc_scr[hh] + pv)
            m_scr[hh] = m_new
            mb_scr[hh] = jnp.max(st_n, axis=0, keepdims=True)

    def trip(t, carry):
        for u in range(0, unroll, 2):
            block_step(unroll * t + u, s_a, s_b, p_a, p_b)
            block_step(unroll * t + u + 1, s_b, s_a, p_b, p_a)
        return carry

    lax.fori_loop(0, n_lat // unroll, trip, 0)
    outs = []
    for hh in range(heads):
        acc = acc_scr[hh] + jnp.dot(vtl_ref[0, hh, n_lat - 1], p_b[hh], preferred_element_type=F32)
        outs.append(acc[:MLA_V] / acc[MLA_V:MLA_V + 1])
    o_ref[0] = jnp.transpose(jnp.concatenate(outs, axis=0)).astype(o_ref.dtype)


def _attention(q, k_ctx, vt_ctx, k_lat, vt_lat, *, tq):
    bsz, nh, seq, _ = q.shape
    heads = LANE // MLA_V
    n_c = k_ctx.shape[2]
    n_lat, _, kb = vt_lat.shape[2:]
    unroll = 4 if n_lat % 4 == 0 else 2
    assert n_lat % unroll == 0
    k_lat = k_lat.reshape(bsz, nh, n_lat, kb, LANE)
    return pl.pallas_call(
        functools.partial(_attn_body, n_lat=n_lat, heads=heads, unroll=unroll),
        grid=(bsz, nh // heads, seq // tq),
        in_specs=[pl.BlockSpec((1, heads, tq, LANE), lambda b, hp, i: (b, hp, i, 0)),
                  pl.BlockSpec((1, heads, n_c, LANE), lambda b, hp, i: (b, hp, 0, 0)),
                  pl.BlockSpec((1, heads, 1, VT_ROWS, n_c), lambda b, hp, i: (b, hp, 0, 0, 0)),
                  pl.BlockSpec((1, heads, n_lat, kb, LANE), lambda b, hp, i: (b, hp, 0, 0, 0)),
                  pl.BlockSpec((1, heads, n_lat, VT_ROWS, kb), lambda b, hp, i: (b, hp, 0, 0, 0))],
        out_specs=pl.BlockSpec((1, tq, LANE), lambda b, hp, i: (b, i, hp)),
        out_shape=jax.ShapeDtypeStruct((bsz, seq, nh * MLA_V), BF16),
        scratch_shapes=[pltpu.VMEM((heads, kb, tq), F32), pltpu.VMEM((heads, kb, tq), F32),
                        pltpu.VMEM((heads, kb, tq), BF16), pltpu.VMEM((heads, kb, tq), BF16),
                        pltpu.VMEM((heads, VT_ROWS, tq), F32), pltpu.VMEM((heads, 1, tq), F32),
                        pltpu.VMEM((heads, 1, tq), F32)],
        compiler_params=_cparams("parallel", "parallel", "arbitrary"),
        name="mla_attention",
    )(q, k_ctx, vt_ctx, k_lat, vt_lat)


def _attn_bounded_body(q_ref, kc_ref, vtc_ref, kl_ref, vtl_ref, shift_ref, o_ref, p_a, p_b, acc_scr,
                       *, n_lat, heads, unroll):
    shift = shift_ref[...].astype(BF16)
    qs = [q_ref[0, hh] + shift for hh in range(heads)]

    def probs(kt, hh):
        st = lax.dot_general(kt, qs[hh], (((1,), (1,)), ((), ())), preferred_element_type=F32)
        return jnp.exp2(st).astype(BF16)

    for hh in range(heads):
        acc_scr[hh] = jnp.dot(vtc_ref[0, hh, 0], probs(kc_ref[0, hh], hh), preferred_element_type=F32)
        p_a[hh] = probs(kl_ref[0, hh, 0], hh)

    def block_step(j, p_cur, p_nxt, produce):
        for hh in range(heads):
            if produce:
                p_nxt[hh] = probs(kl_ref[0, hh, j + 1], hh)
            acc_scr[hh] += jnp.dot(vtl_ref[0, hh, j], p_cur[hh], preferred_element_type=F32)

    def steps(j0, last):
        for u in range(0, unroll, 2):
            block_step(j0 + u, p_a, p_b, True)
            block_step(j0 + u + 1, p_b, p_a, not (last and u + 2 == unroll))

    def trip(t, carry):
        steps(unroll * t, False)
        return carry

    lax.fori_loop(0, n_lat // unroll - 1, trip, 0)
    steps(n_lat - unroll, True)
    outs = [acc_scr[hh][:MLA_V] / acc_scr[hh][MLA_V:MLA_V + 1] for hh in range(heads)]
    o_ref[0] = jnp.transpose(jnp.concatenate(outs, axis=0)).astype(o_ref.dtype)


def _attention_bounded(q, k_ctx, vt_ctx, k_lat, vt_lat, shift, *, tq):
    bsz, nh, seq, _ = q.shape
    heads = LANE // MLA_V
    n_c = k_ctx.shape[2]
    n_lat, _, kb = vt_lat.shape[2:]
    unroll = 4 if n_lat % 4 == 0 else 2
    assert n_lat % unroll == 0
    k_lat = k_lat.reshape(bsz, nh, n_lat, kb, LANE)
    return pl.pallas_call(
        functools.partial(_attn_bounded_body, n_lat=n_lat, heads=heads, unroll=unroll),
        grid=(bsz, nh // heads, seq // tq),
        in_specs=[pl.BlockSpec((1, heads, tq, LANE), lambda b, hp, i: (b, hp, i, 0)),
                  pl.BlockSpec((1, heads, n_c, LANE), lambda b, hp, i: (b, hp, 0, 0)),
                  pl.BlockSpec((1, heads, 1, VT_ROWS, n_c), lambda b, hp, i: (b, hp, 0, 0, 0)),
                  pl.BlockSpec((1, heads, n_lat, kb, LANE), lambda b, hp, i: (b, hp, 0, 0, 0)),
                  pl.BlockSpec((1, heads, n_lat, VT_ROWS, kb), lambda b, hp, i: (b, hp, 0, 0, 0)),
                  pl.BlockSpec((1, LANE), lambda b, hp, i: (0, 0))],
        out_specs=pl.BlockSpec((1, tq, LANE), lambda b, hp, i: (b, i, hp)),
        out_shape=jax.ShapeDtypeStruct((bsz, seq, nh * MLA_V), BF16),
        scratch_shapes=[pltpu.VMEM((heads, kb, tq), BF16), pltpu.VMEM((heads, kb, tq), BF16),
                        pltpu.VMEM((heads, VT_ROWS, tq), F32)],
        compiler_params=_cparams("parallel", "parallel", "arbitrary"),
        name="mla_attention_bounded",
    )(q, k_ctx, vt_ctx, k_lat, vt_lat, shift)


def _pad_heads(w, heads, dim):
    k = w.shape[0]
    w = w.reshape(k, heads, dim)
    return jnp.pad(w, ((0, 0), (0, 0), (0, LANE - dim))).reshape(k, heads * LANE)


def _swap_pairs(a):
    s = a.shape
    return a.reshape(*s[:-1], s[-1] // 2, 2)[..., ::-1].reshape(s)


def _rope_lanes(a, swapped):
    rope = a[..., MLA_NOPE:]
    lo = jnp.zeros_like(a[..., :MLA_NOPE]) if swapped else a[..., :MLA_NOPE]
    mid = _swap_pairs(rope) if swapped else rope
    return jnp.concatenate([lo, mid, jnp.zeros_like(rope)], axis=-1)


def _rope_tables(rows, gain):
    row = jnp.repeat(jnp.arange(rows, dtype=F32), GRID_W)
    col = jnp.tile(jnp.arange(GRID_W, dtype=F32), rows)
    half = MLA_ROPE // 2
    inv_freq = ROPE_BASE ** (-jnp.arange(0, half, 2, dtype=F32) / half)
    ang = jnp.concatenate([row[:, None] * inv_freq, col[:, None] * inv_freq], axis=-1)
    cos = jnp.repeat(jnp.cos(ang), 2, axis=-1)
    sin = jnp.repeat(jnp.sin(ang), 2, axis=-1) * jnp.tile(jnp.array([-1.0, 1.0], F32), MLA_ROPE // 2)
    n = rows * GRID_W
    cos_tab = jnp.concatenate([jnp.ones((n, MLA_NOPE), F32), cos, jnp.zeros((n, MLA_ROPE), F32)], axis=-1)
    sin_tab = jnp.concatenate([jnp.zeros((n, MLA_NOPE), F32), sin, jnp.zeros((n, MLA_ROPE), F32)], axis=-1)
    return cos_tab * _rope_lanes(gain, False), sin_tab * _rope_lanes(gain, True)


def _no_rope_tables(n, gain):
    return jnp.broadcast_to(_rope_lanes(gain, False), (n, LANE)), jnp.zeros((n, LANE), F32)


def _row_tile(seq, want):
    t = min(want, seq)
    assert seq % t == 0 and t % HALO == 0
    return t


def kernel(x, c, ctx, c_ctx, ada_w, ada_b, ab_w_in, ab_gate_bias, ab_conv_w, ab_conv_b, ab_ln_g, ab_ln_b,
           ab_head_gain, ab_w_out, mla_w_in, mla_q_norm, mla_kv_norm, mla_w_uq, mla_w_ukv, mla_q_gain,
           mla_k_gain, mla_w_out, ffn_w_in, ffn_conv_w, ffn_conv_b, ffn_w_out):
    bsz, seq, d = x.shape
    n_ctx = ctx.shape[1]
    assert bsz <= 7 and seq % GRID_W == 0
    tm_l = _row_tile(seq, 512)
    tm_c = _row_tile(n_ctx, 512)
    chunk_l = _row_tile(seq, 256)
    chunk_c = _row_tile(n_ctx, 256)

    c8 = jnp.zeros((8, d), F32).at[:bsz].set(c).at[bsz].set(c_ctx)
    mods = _ada_mods(c8, ada_w, ada_b)

    def mod_vecs(layer):
        parts = jnp.split(mods[layer], 6, axis=-1)
        lat = [p[:bsz, None, :] for p in parts]
        cx = [jnp.broadcast_to(p[bsz][None, None, :], (bsz, 1, d)) for p in parts]
        return lat, cx

    xl = x.reshape(bsz * seq, d)
    xc = ctx.reshape(bsz * n_ctx, d)

    def conv_ffn(xr, mod, layer, seq_len, tm):
        shift, scale, gate = mod[3], mod[4], mod[5]
        return _conv_ffn(xr, 1.0 + scale, shift, gate, ffn_w_in[layer].astype(BF16), ffn_conv_w[layer],
                         ffn_conv_b[layer], ffn_w_out[layer].astype(BF16), seq=seq_len, tm=tm)

    lat, cx = mod_vecs(0)
    w_in = ab_w_in[0]
    wa, wg, wq, wk, wv, wo, wgt = jnp.split(
        w_in, [512, 1024, 1280, 1536, 2048, 2560], axis=1)
    w_main = jnp.concatenate([wa, wg, wv, wo, _pad_heads(wq, MLSTM_HEADS, MLSTM_QK)], axis=1).astype(BF16)
    a_blk, g_blk, v_blk, o_blk, q_blk = 0, 1, 2, 3, 4
    w_kt = jnp.transpose(_pad_heads(wk * (MLSTM_QK ** -0.5), MLSTM_HEADS, MLSTM_QK)).astype(BF16)
    wgi = jnp.concatenate([wgt[:, 0:4], wgt[:, 8:12]], axis=1)
    wgf = jnp.concatenate([wgt[:, 4:8], wgt[:, 12:16]], axis=1)
    pad8 = lambda a: jnp.pad(a, ((0, 0), (0, LANE - 8)))
    wgate = jnp.concatenate([pad8(wgi), pad8(wgf)], axis=1)
    wgate_hi = wgate.astype(BF16)
    wgate_lo = (wgate - wgate_hi.astype(F32)).astype(BF16)
    w_gate = jnp.stack([wgate_hi, wgate_lo])
    gb = ab_gate_bias[0]
    gbias = jnp.concatenate([pad8(jnp.concatenate([gb[0:4], gb[8:12]])[None, :]),
                             pad8(jnp.concatenate([gb[4:8], gb[12:16]])[None, :])], axis=1)
    w_out0 = ab_w_out[0].astype(BF16)

    def mixer0(xr, mod, seq_len, tm, chunk, s0, m0):
        y, kt, gates = _ab_in(xr, 1.0 + mod[1], mod[0], w_main, w_kt, w_gate, bsz=bsz, seq=seq_len, tm=tm)
        gc, gr = _gate_prep(gates, gbias, bsz=bsz, seq=seq_len, chunk=chunk)
        hf, hb, s1, m1 = _mlstm(y, kt, gc, gr, s0, m0, bsz=bsz, seq=seq_len, chunk=chunk,
                                q_blk=q_blk, v_blk=v_blk)
        out = _merge(y, hf.reshape(-1, hf.shape[-1]), hb.reshape(-1, hb.shape[-1]), ab_conv_w[0], ab_conv_b[0],
                     ab_ln_g[0], ab_ln_b[0], ab_head_gain[0], xr, mod[2], w_out0, seq=seq_len, tm=tm,
                     a_blk=a_blk, g_blk=g_blk, o_blk=o_blk)
        return out, s1, m1

    nch = 2 * MLSTM_HEADS
    s_zero = jnp.zeros((bsz, nch, LANE, 2 * LANE), F32)
    m_zero = jnp.zeros((bsz, nch, 8, LANE), F32)
    xc, s_ctx, m_ctx = mixer0(xc, cx, n_ctx, tm_c, chunk_c, s_zero, m_zero)
    xl, _, _ = mixer0(xl, lat, seq, tm_l, chunk_l, s_ctx, m_ctx)
    xl = conv_ffn(xl, lat, 0, seq, tm_l)
    xc = conv_ffn(xc, cx, 0, n_ctx, tm_c)

    lat, cx = mod_vecs(1)
    w_in = mla_w_in[0]
    w_kr = w_in[:, MLA_Q_LORA + MLA_KV_LORA:]
    zpad = lambda n: jnp.zeros((d, n), F32)
    w_in1 = jnp.concatenate([w_in[:, :MLA_Q_LORA + MLA_KV_LORA],
                             zpad(MLA_NOPE), w_kr, zpad(MLA_ROPE),
                             zpad(MLA_NOPE), _swap_pairs(w_kr), zpad(MLA_ROPE)], axis=1).astype(BF16)
    w_uq = mla_w_uq[0].reshape(MLA_Q_LORA, MLA_HEADS, MLA_QK)
    w_uq_a = _rope_lanes(w_uq, False).reshape(MLA_Q_LORA, MLA_HEADS * LANE).astype(BF16)
    w_uq_b = _rope_lanes(w_uq, True).reshape(MLA_Q_LORA, MLA_HEADS * LANE).astype(BF16)
    w_ukv = mla_w_ukv[0].reshape(MLA_KV_LORA, MLA_HEADS, MLA_NOPE + MLA_V)
    w_uk = _pad_heads(w_ukv[..., :MLA_NOPE].reshape(MLA_KV_LORA, -1), MLA_HEADS, MLA_NOPE).astype(BF16)
    w_uvt = jnp.transpose(w_ukv[..., MLA_NOPE:].reshape(MLA_KV_LORA, -1)).astype(BF16)
    ind = (jnp.arange(MLA_HEADS * LANE)[:, None] // LANE == jnp.arange(LANE)[None, :]).astype(BF16)
    q_tabs = _rope_tables(seq // GRID_W, mla_q_gain[0] * (MLA_QK ** -0.5 * math.log2(math.e)))
    k_tabs = _rope_tables(seq // GRID_W, mla_k_gain[0])
    kc_tabs = _no_rope_tables(n_ctx, mla_k_gain[0])

    tk = 256
    kv_blk = min(1024, seq // 2)
    assert seq % kv_blk == 0 and kv_blk % tk == 0 and n_ctx % tk == 0
    c_ctx_ = _mm(xc, w_in1, tm=tm_c, out_dtype=F32, group_tiles=n_ctx // tm_c,
                 pro=(1.0 + cx[1], cx[0]), name="mla_in")
    q, k_lat, vt_lat = _mla_prep(xl, 1.0 + lat[1], lat[0], w_in1, mla_q_norm[0], mla_kv_norm[0], w_uq_a, w_uq_b,
                                 w_uk, w_uvt, ind, q_tabs, k_tabs, bsz=bsz, seq=seq, tm=tk, vt_width=kv_blk)
    k_ctx, vt_ctx = _kv_prep(c_ctx_, mla_kv_norm[0], w_uk, w_uvt, ind, *kc_tabs, bsz=bsz, seq=n_ctx, tm=tk,
                             vt_width=n_ctx)
    bound = (1.02 * MLA_QK ** 0.5 * math.log2(math.e)) * jnp.max(jnp.abs(mla_q_gain[0])) * jnp.max(
        jnp.abs(mla_k_gain[0]))
    shift = jnp.zeros((1, LANE), F32).at[0, MLA_QK].set(-bound)
    att = lax.cond(
        2.0 * bound < MAX_EXP2_SPAN,
        lambda: _attention_bounded(q, k_ctx, vt_ctx, k_lat, vt_lat, shift, tq=min(2048, seq)),
        lambda: _attention(q, k_ctx, vt_ctx, k_lat, vt_lat, tq=min(512, seq)))
    xl = _mm(att.reshape(bsz * seq, -1), mla_w_out[0].astype(BF16), tm=tm_l, out_dtype=F32,
             group_tiles=seq // tm_l, res=(xl, lat[2]), name="mla_out")
    xl = conv_ffn(xl, lat, 1, seq, tm_l)
    return xl.reshape(bsz, seq, d)
```

```python
import functools
import math

import jax
import jax.numpy as jnp
from jax import lax
from jax.experimental import pallas as pl
from jax.experimental.pallas import tpu as pltpu

F32 = jnp.float32
BF16 = jnp.bfloat16

EPS = 1e-6
GRID_W = 64
CONV_DIM = 512
CONV_WIDTH = 31
MLSTM_HEADS = 4
MLSTM_QK = 64
MLSTM_V = 128
MLA_HEADS = 16
MLA_Q_LORA = 512
MLA_KV_LORA = 256
MLA_NOPE = 64
MLA_ROPE = 32
MLA_V = 64
MLA_QK = MLA_NOPE + MLA_ROPE
ROPE_BASE = 10000.0
FFN_DIM = 2816

LANE = 128
SUBLANE = 8
HALO = 16
CONV_ROWS = 64
MAX_EXP2_SPAN = 100.0
VT_ROWS = MLA_V + 16
VMEM_LIMIT = 48 * 1024 * 1024


def _cparams(*sem):
    return pltpu.CompilerParams(dimension_semantics=sem, vmem_limit_bytes=VMEM_LIMIT)


def _const_spec(shape):
    nd = len(shape)
    return pl.BlockSpec(shape, lambda *_: (0,) * nd, pipeline_mode=pl.Buffered(1))


def _split3(a):
    hi = a.astype(BF16)
    r = a - hi.astype(F32)
    mid = r.astype(BF16)
    lo = (r - mid.astype(F32)).astype(BF16)
    return hi, mid, lo


def _rms_rows(xf):
    return xf * lax.rsqrt(jnp.mean(xf * xf, axis=-1, keepdims=True) + EPS)


def _ada_body(c_ref, w_ref, b_ref, o_ref):
    c = c_ref[...]
    s = c * jax.nn.sigmoid(c)
    s_hi, s_lo, _ = _split3(s)
    w = w_ref[0]
    w_hi = w.astype(BF16)
    w_lo = (w - w_hi.astype(F32)).astype(BF16)
    acc = jnp.dot(s_hi, w_hi, preferred_element_type=F32)
    acc += jnp.dot(s_hi, w_lo, preferred_element_type=F32)
    acc += jnp.dot(s_lo, w_hi, preferred_element_type=F32)
    o_ref[0] = acc + b_ref[0]


def _ada_mods(c8, ada_w, ada_b):
    depth, d, n = ada_w.shape
    tn = 1536
    return pl.pallas_call(
        _ada_body,
        grid=(depth, n // tn),
        in_specs=[pl.BlockSpec((8, d), lambda l, j: (0, 0)),
                  pl.BlockSpec((1, d, tn), lambda l, j: (l, 0, j)),
                  pl.BlockSpec((1, 1, tn), lambda l, j: (l, 0, j))],
        out_specs=pl.BlockSpec((1, 8, tn), lambda l, j: (l, 0, j)),
        out_shape=jax.ShapeDtypeStruct((depth, 8, n), F32),
        compiler_params=_cparams("parallel", "parallel"),
        name="ada_mods",
    )(c8, ada_w, ada_b.reshape(depth, 1, n))


def _mm_body(*refs, has_pro, has_res, n_chunk):
    it = iter(refs)
    x_ref = next(it)
    if has_pro:
        a_ref, b_ref = next(it), next(it)
    w_ref = next(it)
    if has_res:
        r_ref, g_ref = next(it), next(it)
    o_ref = next(it)
    if has_pro:
        h = (_rms_rows(x_ref[...].astype(F32)) * a_ref[0] + b_ref[0]).astype(BF16)
    else:
        h = x_ref[...].astype(BF16)
    n = o_ref.shape[-1]
    for c0 in range(0, n, n_chunk):
        c1 = min(c0 + n_chunk, n)
        acc = jnp.dot(h, w_ref[:, c0:c1], preferred_element_type=F32)
        if has_res:
            acc = r_ref[:, c0:c1] + g_ref[0][:, c0:c1] * acc
        o_ref[:, c0:c1] = acc.astype(o_ref.dtype)


def _mm(x, w, *, tm, out_dtype, group_tiles, pro=None, res=None, x_cols=None, n_chunk=512, name):
    m = x.shape[0]
    k, n = w.shape
    xcol = 0 if x_cols is None else x_cols
    in_specs = [pl.BlockSpec((tm, k), lambda i: (i, xcol))]
    args = [x]
    if pro is not None:
        in_specs += [pl.BlockSpec((1, 1, k), lambda i: (i // group_tiles, 0, 0))] * 2
        args += list(pro)
    in_specs.append(_const_spec((k, n)))
    args.append(w)
    if res is not None:
        in_specs += [pl.BlockSpec((tm, n), lambda i: (i, 0)),
                     pl.BlockSpec((1, 1, n), lambda i: (i // group_tiles, 0, 0))]
        args += list(res)
    return pl.pallas_call(
        functools.partial(_mm_body, has_pro=pro is not None, has_res=res is not None, n_chunk=n_chunk),
        grid=(m // tm,),
        in_specs=in_specs,
        out_specs=pl.BlockSpec((tm, n), lambda i: (i, 0)),
        out_shape=jax.ShapeDtypeStruct((m, n), out_dtype),
        compiler_params=_cparams("parallel"),
        name=name,
    )(*args)


def _ab_in_body(x_ref, a_ref, b_ref, w_ref, wkt_ref, wg_ref, y_ref, kt_ref, g_ref, *, n_chunk):
    hf = _rms_rows(x_ref[...]) * a_ref[0] + b_ref[0]
    h = hf.astype(BF16)
    a = jnp.dot(h, w_ref[:, :CONV_DIM], preferred_element_type=F32)
    g = jnp.dot(h, w_ref[:, CONV_DIM:2 * CONV_DIM], preferred_element_type=F32)
    y_ref[:, :CONV_DIM] = (a * jax.nn.sigmoid(g)).astype(y_ref.dtype)
    n = w_ref.shape[-1]
    for c0 in range(2 * CONV_DIM, n, n_chunk):
        c1 = min(c0 + n_chunk, n)
        y_ref[:, c0 - CONV_DIM:c1 - CONV_DIM] = jnp.dot(
            h, w_ref[:, c0:c1], preferred_element_type=F32).astype(y_ref.dtype)
    kt_ref[0] = lax.dot_general(wkt_ref[...], h, (((1,), (1,)), ((), ())),
                                preferred_element_type=F32).astype(kt_ref.dtype)
    h_hi, h_lo, _ = _split3(hf)
    acc = jnp.dot(h_hi, wg_ref[0], preferred_element_type=F32)
    acc += jnp.dot(h_hi, wg_ref[1], preferred_element_type=F32)
    acc += jnp.dot(h_lo, wg_ref[0], preferred_element_type=F32)
    g_ref[...] = acc


def _ab_in(x, mod_a, mod_b, w_main, w_kt, w_gate, *, bsz, seq, tm):
    m, d = x.shape
    n = w_main.shape[1] - CONV_DIM
    tiles = seq // tm
    return pl.pallas_call(
        functools.partial(_ab_in_body, n_chunk=512),
        grid=(m // tm,),
        in_specs=[pl.BlockSpec((tm, d), lambda i: (i, 0)),
                  pl.BlockSpec((1, 1, d), lambda i: (i // tiles, 0, 0)),
                  pl.BlockSpec((1, 1, d), lambda i: (i // tiles, 0, 0)),
                  _const_spec(w_main.shape), _const_spec(w_kt.shape), _const_spec(w_gate.shape)],
        out_specs=[pl.BlockSpec((tm, n), lambda i: (i, 0)),
                   pl.BlockSpec((1, w_kt.shape[0], tm), lambda i: (i // tiles, 0, i % tiles)),
                   pl.BlockSpec((tm, 2 * LANE), lambda i: (i, 0))],
        out_shape=[jax.ShapeDtypeStruct((m, n), BF16),
                   jax.ShapeDtypeStruct((bsz, w_kt.shape[0], seq), BF16),
                   jax.ShapeDtypeStruct((m, 2 * LANE), F32)],
        compiler_params=_cparams("parallel"),
        name="ab_in",
    )(x, mod_a, mod_b, w_main, w_kt, w_gate)


def _gate_prep_body(g_ref, bias_ref, gc_ref, gr_ref, *, chunk):
    g = g_ref[0]
    li = g[:, :LANE] + bias_ref[:, :LANE]
    lf = jax.nn.log_sigmoid(g[:, LANE:] + bias_ref[:, LANE:])
    row = lax.broadcasted_iota(jnp.int32, (chunk, chunk), 0)
    col = lax.broadcasted_iota(jnp.int32, (chunk, chunk), 1)
    lower = (col <= row).astype(BF16)
    upper = (col >= row).astype(BF16)
    b_f = jnp.zeros((chunk, LANE), F32)
    b_b = jnp.zeros((chunk, LANE), F32)
    for piece in _split3(lf):
        b_f += jnp.dot(lower, piece, preferred_element_type=F32)
        b_b += jnp.dot(upper, piece, preferred_element_type=F32)
    lane = lax.broadcasted_iota(jnp.int32, (chunk, LANE), 1)
    fwd = lane < MLSTM_HEADS
    b = jnp.where(fwd, b_f, b_b)
    r = li - b
    tok = lax.broadcasted_iota(jnp.int32, (chunk, LANE), 0)
    rm_f, rm_b = r, r
    step = 1
    while step < chunk:
        rm_f = jnp.maximum(rm_f, jnp.where(tok >= step, pltpu.roll(rm_f, step, axis=0), -jnp.inf))
        rm_b = jnp.maximum(rm_b, jnp.where(tok < chunk - step, pltpu.roll(rm_b, chunk - step, axis=0), -jnp.inf))
        step *= 2
    gc_ref[0, :, :LANE] = b
    gc_ref[0, :, LANE:] = jnp.where(fwd, rm_f, rm_b)
    gr_ref[0] = jnp.transpose(r)[:8, :]


def _gate_prep(gates, bias, *, bsz, seq, chunk):
    g3 = gates.reshape(bsz, seq, 2 * LANE)
    nc = seq // chunk
    return pl.pallas_call(
        functools.partial(_gate_prep_body, chunk=chunk),
        grid=(bsz, nc),
        in_specs=[pl.BlockSpec((1, chunk, 2 * LANE), lambda b, c: (b, c, 0)),
                  pl.BlockSpec((1, 2 * LANE), lambda b, c: (0, 0))],
        out_specs=[pl.BlockSpec((1, chunk, 2 * LANE), lambda b, c: (b, c, 0)),
                   pl.BlockSpec((1, 8, chunk), lambda b, c: (b, 0, c))],
        out_shape=[jax.ShapeDtypeStruct((bsz, seq, 2 * LANE), F32),
                   jax.ShapeDtypeStruct((bsz, 8, seq), F32)],
        compiler_params=_cparams("parallel", "parallel"),
        name="gate_prep",
    )(g3, bias)


def _mlstm_body(qf_ref, vf_ref, ktf_ref, gcf_ref, grf_ref,
                qb_ref, vb_ref, ktb_ref, gcb_ref, grb_ref,
                s0_ref, m0_ref,
                hf_ref, hb_ref, s1_ref, m1_ref,
                s_scr, m_scr, *, chunk):
    c = pl.program_id(1)
    nc = pl.num_programs(1)

    @pl.when(c == 0)
    def _():
        s_scr[...] = s0_ref[0]
        m_scr[...] = m0_ref[0]

    row = lax.broadcasted_iota(jnp.int32, (chunk, chunk), 0)
    col = lax.broadcasted_iota(jnp.int32, (chunk, chunk), 1)
    ones_col = (lax.broadcasted_iota(jnp.int32, (chunk, LANE), 1) == 0).astype(BF16)

    for d in range(2):
        q_ref, v_ref, kt_ref, gc_ref, gr_ref, h_ref = (
            (qf_ref, vf_ref, ktf_ref, gcf_ref, grf_ref, hf_ref) if d == 0 else
            (qb_ref, vb_ref, ktb_ref, gcb_ref, grb_ref, hb_ref))
        mask = (col <= row) if d == 0 else (col >= row)
        for hd in range(MLSTM_HEADS):
            ch = d * MLSTM_HEADS + hd
            sl = slice(hd * LANE, (hd + 1) * LANE)
            q = q_ref[0][:, sl]
            kt = kt_ref[0][sl, :]
            v = v_ref[0][:, sl]
            bcol = gc_ref[0][:, ch:ch + 1]
            rmcol = gc_ref[0][:, LANE + ch:LANE + ch + 1]
            rrow = gr_ref[0][ch:ch + 1, :]
            m_old = m_scr[ch][0:1, 0:1]
            b_last = bcol[chunk - 1:chunk, :] if d == 0 else bcol[0:1, :]
            cm = jnp.maximum(rmcol, m_old)
            s = jnp.dot(q, kt, preferred_element_type=F32) * jnp.exp(jnp.where(mask, rrow - cm, -jnp.inf))
            inter = jnp.exp(m_old - cm)
            v_ext = jnp.concatenate([v, ones_col], axis=1)
            st = s_scr[ch]
            nd = (jnp.dot(s.astype(BF16), v_ext, preferred_element_type=F32)
                  + inter * jnp.dot(q, st.astype(BF16), preferred_element_type=F32))
            num = nd[:, :LANE]
            den = nd[:, LANE:LANE + 1]
            h_ref[0, :, sl] = num * (1.0 / jnp.maximum(jnp.abs(den), jnp.exp(-(bcol + cm))))
            log_w = b_last + rrow
            m_new = jnp.maximum(b_last + m_old, jnp.max(log_w, axis=1, keepdims=True))
            w = jnp.exp(log_w - m_new)
            decay = jnp.exp(b_last + m_old - m_new)
            kw = (kt.astype(F32) * w).astype(BF16)
            s_scr[ch] = decay * st + jnp.dot(kw, v_ext, preferred_element_type=F32)
            m_scr[ch] = jnp.broadcast_to(m_new, (8, LANE))

    @pl.when(c == nc - 1)
    def _():
        s1_ref[0] = s_scr[...]
        m1_ref[0] = m_scr[...]


def _mlstm(y, kt, gc, gr, s0, m0, *, bsz, seq, chunk, q_blk, v_blk):
    nc = seq // chunk
    width = MLSTM_HEADS * LANE
    y3 = y.reshape(bsz, seq, y.shape[-1])
    nch = 2 * MLSTM_HEADS

    def fwd(b, c):
        return c

    def bwd(b, c):
        return nc - 1 - c

    def specs(pos):
        return [pl.BlockSpec((1, chunk, width), lambda b, c: (b, pos(b, c), q_blk)),
                pl.BlockSpec((1, chunk, width), lambda b, c: (b, pos(b, c), v_blk)),
                pl.BlockSpec((1, width, chunk), lambda b, c: (b, 0, pos(b, c))),
                pl.BlockSpec((1, chunk, 2 * LANE), lambda b, c: (b, pos(b, c), 0)),
                pl.BlockSpec((1, 8, chunk), lambda b, c: (b, 0, pos(b, c)))]

    state_specs = [pl.BlockSpec((1, nch, LANE, 2 * LANE), lambda b, c: (b, 0, 0, 0)),
                   pl.BlockSpec((1, nch, 8, LANE), lambda b, c: (b, 0, 0, 0))]
    return pl.pallas_call(
        functools.partial(_mlstm_body, chunk=chunk),
        grid=(bsz, nc),
        in_specs=specs(fwd) + specs(bwd) + state_specs,
        out_specs=[pl.BlockSpec((1, chunk, width), lambda b, c: (b, c, 0)),
                   pl.BlockSpec((1, chunk, width), lambda b, c: (b, nc - 1 - c, 0))] + state_specs,
        out_shape=[jax.ShapeDtypeStruct((bsz, seq, width), F32),
                   jax.ShapeDtypeStruct((bsz, seq, width), F32),
                   jax.ShapeDtypeStruct(s0.shape, F32),
                   jax.ShapeDtypeStruct(m0.shape, F32)],
        scratch_shapes=[pltpu.VMEM((nch, LANE, 2 * LANE), F32), pltpu.VMEM((nch, 8, LANE), F32)],
        compiler_params=_cparams("parallel", "arbitrary"),
        name="mlstm",
    )(y3, y3, kt, gc, gr, y3, y3, kt, gc, gr, s0, m0)


def _merge_body(uc_ref, up_ref, un_ref, o_ref, hf_ref, hb_ref,
                cw_ref, cb_ref, lg_ref, lb_ref, hg_ref, x_ref, gate_ref, wout_ref, out_ref,
                z_ref, scr, sh_scr, *, tm, tiles):
    i = pl.program_id(0)
    not_first = (i % tiles != 0).astype(F32)
    not_last = (i % tiles != tiles - 1).astype(F32)
    scr[0:HALO, :] = up_ref[...].astype(F32) * not_first
    scr[HALO:HALO + tm, :] = uc_ref[...].astype(F32)
    scr[HALO + tm:2 * HALO + tm, :] = un_ref[...].astype(F32) * not_last
    rows = tm + 2 * HALO
    for r in range(SUBLANE):
        sh_scr[r, 0:rows - SUBLANE, :] = scr[r:rows - SUBLANE + r, :]
    first = HALO - (CONV_WIDTH - 1) // 2

    def row_block(rb, carry):
        base = pl.multiple_of(rb * CONV_ROWS, CONV_ROWS)
        acc = jnp.zeros((CONV_ROWS, CONV_DIM), F32)
        for k in range(CONV_WIDTH):
            off = first + k
            acc += sh_scr[off % SUBLANE, pl.ds(base + off - off % SUBLANE, CONV_ROWS), :] * cw_ref[k:k + 1, :]
        acc += cb_ref[...]
        mu = jnp.mean(acc, axis=-1, keepdims=True)
        cen = acc - mu
        var = jnp.mean(cen * cen, axis=-1, keepdims=True)
        u = cen * lax.rsqrt(var + EPS) * lg_ref[...] + lb_ref[...]
        z_ref[pl.ds(base, CONV_ROWS), :CONV_DIM] = (u * jax.nn.sigmoid(u)).astype(z_ref.dtype)
        return carry

    lax.fori_loop(0, tm // CONV_ROWS, row_block, 0)
    hs = hf_ref[...] + hb_ref[...]
    og = hg_ref[...] * jax.nn.sigmoid(o_ref[...].astype(F32))
    for hd in range(MLSTM_HEADS):
        sl = slice(hd * MLSTM_V, (hd + 1) * MLSTM_V)
        z_ref[:, CONV_DIM + hd * MLSTM_V:CONV_DIM + (hd + 1) * MLSTM_V] = (
            _rms_rows(hs[:, sl]) * og[:, sl]).astype(z_ref.dtype)
    out_ref[...] = x_ref[...] + gate_ref[0] * jnp.dot(z_ref[...], wout_ref[...], preferred_element_type=F32)


def _merge(y, hf, hb, conv_w, conv_b, ln_g, ln_b, head_gain, x, gate, w_out, *, seq, tm, u_blk, o_blk):
    m = y.shape[0]
    d = x.shape[1]
    assert tm % CONV_ROWS == 0
    tiles = seq // tm
    r = tm // HALO
    last = m // HALO - 1
    cur = lambda blk: pl.BlockSpec((tm, CONV_DIM), lambda i: (i, blk))
    prev = lambda blk: pl.BlockSpec((HALO, CONV_DIM), lambda i: (jnp.maximum(i * r - 1, 0), blk))
    nxt = lambda blk: pl.BlockSpec((HALO, CONV_DIM), lambda i: (jnp.minimum((i + 1) * r, last), blk))
    vec = lambda a: a.reshape(1, -1)
    hspec = pl.BlockSpec((tm, CONV_DIM), lambda i: (i, 0))
    return pl.pallas_call(
        functools.partial(_merge_body, tm=tm, tiles=tiles),
        grid=(m // tm,),
        in_specs=[cur(u_blk), prev(u_blk), nxt(u_blk), cur(o_blk),
                  hspec, hspec,
                  _const_spec((CONV_WIDTH, CONV_DIM)), _const_spec((1, CONV_DIM)), _const_spec((1, CONV_DIM)),
                  _const_spec((1, CONV_DIM)), _const_spec((1, CONV_DIM)),
                  pl.BlockSpec((tm, d), lambda i: (i, 0)), pl.BlockSpec((1, 1, d), lambda i: (i // tiles, 0, 0)),
                  _const_spec(w_out.shape)],
        out_specs=pl.BlockSpec((tm, d), lambda i: (i, 0)),
        out_shape=jax.ShapeDtypeStruct((m, d), F32),
        scratch_shapes=[pltpu.VMEM((tm, 2 * CONV_DIM), BF16),
                        pltpu.VMEM((tm + 2 * HALO, CONV_DIM), F32),
                        pltpu.VMEM((SUBLANE, tm + 2 * HALO, CONV_DIM), F32)],
        compiler_params=_cparams("parallel"),
        name="ab_merge_out",
    )(y, y, y, y, hf, hb, conv_w, vec(conv_b), vec(ln_g), vec(ln_b), vec(head_gain), x, gate, w_out)


def _ffn_body(xc_ref, xp_ref, xn_ref, a_ref, b_ref, gate_ref, win_ref, cw_ref, cb_ref, wout_ref, o_ref,
              h_scr, u_scr, *, tm, tiles, f_chunk):
    i = pl.program_id(0)
    not_first = (i % tiles != 0).astype(F32)
    not_last = (i % tiles != tiles - 1).astype(F32)

    def modulate(x):
        return _rms_rows(x) * a_ref[0] + b_ref[0]

    h_scr[0:HALO, :] = (modulate(xp_ref[...]) * not_first).astype(BF16)
    h_scr[HALO:HALO + tm, :] = modulate(xc_ref[...]).astype(BF16)
    h_scr[HALO + tm:, :] = (modulate(xn_ref[...]) * not_last).astype(BF16)
    f = cw_ref.shape[1]
    for c0 in range(0, f, f_chunk):
        c1 = min(c0 + f_chunk, f)
        g = jnp.dot(h_scr[...], win_ref[:, c0:c1], preferred_element_type=F32)
        v = jnp.dot(h_scr[HALO:HALO + tm, :], win_ref[:, f + c0:f + c1], preferred_element_type=F32)
        y = (g[HALO - 1:HALO - 1 + tm] * cw_ref[0:1, c0:c1] + g[HALO:HALO + tm] * cw_ref[1:2, c0:c1]
             + g[HALO + 1:HALO + 1 + tm] * cw_ref[2:3, c0:c1] + cb_ref[:, c0:c1])
        u_scr[:, c0:c1] = (jax.nn.gelu(y, approximate=True) * v).astype(BF16)
    o_ref[...] = xc_ref[...] + gate_ref[0] * jnp.dot(u_scr[...], wout_ref[...], preferred_element_type=F32)


def _conv_ffn(x, mod_a, mod_b, gate, w_in, conv_w, conv_b, w_out, *, seq, tm):
    m, d = x.shape
    f = conv_w.shape[1]
    tiles = seq // tm
    r = tm // HALO
    last = m // HALO - 1
    grp = lambda i: (i // tiles, 0, 0)
    return pl.pallas_call(
        functools.partial(_ffn_body, tm=tm, tiles=tiles, f_chunk=256),
        grid=(m // tm,),
        in_specs=[pl.BlockSpec((tm, d), lambda i: (i, 0)),
                  pl.BlockSpec((HALO, d), lambda i: (jnp.maximum(i * r - 1, 0), 0)),
                  pl.BlockSpec((HALO, d), lambda i: (jnp.minimum((i + 1) * r, last), 0)),
                  pl.BlockSpec((1, 1, d), grp), pl.BlockSpec((1, 1, d), grp), pl.BlockSpec((1, 1, d), grp),
                  _const_spec(w_in.shape), _const_spec((3, f)), _const_spec((1, f)), _const_spec(w_out.shape)],
        out_specs=pl.BlockSpec((tm, d), lambda i: (i, 0)),
        out_shape=jax.ShapeDtypeStruct((m, d), F32),
        scratch_shapes=[pltpu.VMEM((tm + 2 * HALO, d), BF16), pltpu.VMEM((tm, f), BF16)],
        compiler_params=_cparams("parallel"),
        name="conv_ffn",
    )(x, x, x, mod_a, mod_b, gate, w_in, conv_w, conv_b.reshape(1, f), w_out)


def _head_rsqrt(xa, ind_ref):
    sq = (xa * xa).astype(BF16)
    ms = jnp.dot(sq, ind_ref[...], preferred_element_type=F32) * (1.0 / MLA_QK)
    return lax.rsqrt(ms + EPS)


def _q_heads(cq, qn_ref, wa_ref, wb_ref, ind_ref, gc_ref, gs_ref, q_ref):
    h = (_rms_rows(cq) * qn_ref[...]).astype(BF16)
    xa = jnp.dot(h, wa_ref[...], preferred_element_type=F32)
    xb = jnp.dot(h, wb_ref[...], preferred_element_type=F32)
    rs = _head_rsqrt(xa, ind_ref)
    gc, gs = gc_ref[...], gs_ref[...]
    for hd in range(MLA_HEADS):
        sl = slice(hd * LANE, (hd + 1) * LANE)
        q_ref[0, hd] = (rs[:, hd:hd + 1] * (xa[:, sl] * gc + xb[:, sl] * gs)).astype(q_ref.dtype)


def _kv_heads(ckv, kra, krb, kn_ref, wk_ref, wvt_ref, ind_ref, gc_ref, gs_ref, k_ref, vt_ref):
    h = (_rms_rows(ckv) * kn_ref[...]).astype(BF16)
    gc = gc_ref[...]
    xa = jnp.dot(h, wk_ref[...], preferred_element_type=F32)
    xa = xa + jnp.concatenate([kra] * MLA_HEADS, axis=1)
    rs = _head_rsqrt(xa, ind_ref)
    rot = krb * gs_ref[...]
    one96 = (lax.broadcasted_iota(jnp.int32, (1, LANE), 1) == MLA_QK).astype(F32)
    for hd in range(MLA_HEADS):
        sl = slice(hd * LANE, (hd + 1) * LANE)
        k_ref[0, hd] = (rs[:, hd:hd + 1] * (xa[:, sl] * gc + rot) + one96).astype(k_ref.dtype)
    vt = lax.dot_general(wvt_ref[...], h, (((1,), (1,)), ((), ())), preferred_element_type=F32)
    tm = vt.shape[1]
    ones_rows = (lax.broadcasted_iota(jnp.int32, (VT_ROWS - MLA_V, tm), 0) == 0).astype(vt_ref.dtype)
    for hd in range(MLA_HEADS):
        vt_ref[0, hd, 0, :MLA_V, :] = vt[hd * MLA_V:(hd + 1) * MLA_V, :].astype(vt_ref.dtype)
        vt_ref[0, hd, 0, MLA_V:, :] = ones_rows


def _mla_prep_body(x_ref, a_ref, b_ref, win_ref, qn_ref, kn_ref, wa_ref, wb_ref, wk_ref, wvt_ref, ind_ref,
                   qgc_ref, qgs_ref, kgc_ref, kgs_ref, q_ref, k_ref, vt_ref):
    h = (_rms_rows(x_ref[...]) * a_ref[0] + b_ref[0]).astype(BF16)
    c = jnp.dot(h, win_ref[...], preferred_element_type=F32)
    kv0 = MLA_Q_LORA
    kr0 = MLA_Q_LORA + MLA_KV_LORA
    _q_heads(c[:, :kv0], qn_ref, wa_ref, wb_ref, ind_ref, qgc_ref, qgs_ref, q_ref)
    _kv_heads(c[:, kv0:kr0], c[:, kr0:kr0 + LANE], c[:, kr0 + LANE:], kn_ref, wk_ref, wvt_ref, ind_ref,
              kgc_ref, kgs_ref, k_ref, vt_ref)


def _mla_prep(x, mod_a, mod_b, w_in, q_norm, kv_norm, w_a, w_b, w_uk, w_uvt, ind, q_tabs, k_tabs,
              *, bsz, seq, tm, vt_width):
    d = x.shape[1]
    tiles = seq // tm
    sub = vt_width // tm
    tab = pl.BlockSpec((tm, LANE), lambda i: (i % tiles, 0))
    grp = pl.BlockSpec((1, 1, d), lambda i: (i // tiles, 0, 0))
    head_major = pl.BlockSpec((1, MLA_HEADS, tm, LANE), lambda i: (i // tiles, 0, i % tiles, 0))
    return pl.pallas_call(
        _mla_prep_body,
        grid=(bsz * tiles,),
        in_specs=[pl.BlockSpec((tm, d), lambda i: (i, 0)), grp, grp, _const_spec(w_in.shape),
                  _const_spec((1, MLA_Q_LORA)), _const_spec((1, MLA_KV_LORA)),
                  _const_spec(w_a.shape), _const_spec(w_b.shape), _const_spec(w_uk.shape), _const_spec(w_uvt.shape),
                  _const_spec(ind.shape), tab, tab, tab, tab],
        out_specs=[head_major, head_major,
                   pl.BlockSpec((1, MLA_HEADS, 1, VT_ROWS, tm),
                                lambda i: (i // tiles, 0, (i % tiles) // sub, 0, (i % tiles) % sub))],
        out_shape=[jax.ShapeDtypeStruct((bsz, MLA_HEADS, seq, LANE), BF16),
                   jax.ShapeDtypeStruct((bsz, MLA_HEADS, seq, LANE), BF16),
                   jax.ShapeDtypeStruct((bsz, MLA_HEADS, seq // vt_width, VT_ROWS, vt_width), BF16)],
        compiler_params=_cparams("parallel"),
        name="mla_prep",
    )(x, mod_a, mod_b, w_in, q_norm.reshape(1, -1), kv_norm.reshape(1, -1), w_a, w_b, w_uk, w_uvt, ind,
      *q_tabs, *k_tabs)


def _kv_prep_body(ckv_ref, kra_ref, krb_ref, kn_ref, wk_ref, wvt_ref, ind_ref, gc_ref, gs_ref,
                  k_ref, vt_ref):
    _kv_heads(ckv_ref[...], kra_ref[...], krb_ref[...], kn_ref, wk_ref, wvt_ref, ind_ref,
              gc_ref, gs_ref, k_ref, vt_ref)


def _kv_prep(ckv, kv_norm, w_uk, w_uvt, ind, gcos, gsin, *, bsz, seq, tm, vt_width):
    tiles = seq // tm
    sub = vt_width // tm
    ckv_blk = MLA_Q_LORA // MLA_KV_LORA
    kra_blk = (MLA_Q_LORA + MLA_KV_LORA) // LANE
    tab = pl.BlockSpec((tm, LANE), lambda i: (i % tiles, 0))
    return pl.pallas_call(
        _kv_prep_body,
        grid=(bsz * tiles,),
        in_specs=[pl.BlockSpec((tm, MLA_KV_LORA), lambda i: (i, ckv_blk)),
                  pl.BlockSpec((tm, LANE), lambda i: (i, kra_blk)),
                  pl.BlockSpec((tm, LANE), lambda i: (i, kra_blk + 1)),
                  _const_spec((1, MLA_KV_LORA)), _const_spec(w_uk.shape), _const_spec(w_uvt.shape),
                  _const_spec(ind.shape), tab, tab],
        out_specs=[pl.BlockSpec((1, MLA_HEADS, tm, LANE), lambda i: (i // tiles, 0, i % tiles, 0)),
                   pl.BlockSpec((1, MLA_HEADS, 1, VT_ROWS, tm),
                                lambda i: (i // tiles, 0, (i % tiles) // sub, 0, (i % tiles) % sub))],
        out_shape=[jax.ShapeDtypeStruct((bsz, MLA_HEADS, seq, LANE), BF16),
                   jax.ShapeDtypeStruct((bsz, MLA_HEADS, seq // vt_width, VT_ROWS, vt_width), BF16)],
        compiler_params=_cparams("parallel"),
        name="mla_kv_prep",
    )(ckv, ckv, ckv, kv_norm.reshape(1, -1), w_uk, w_uvt, ind, gcos, gsin)


def _attn_body(q_ref, kc_ref, vtc_ref, kl_ref, vtl_ref, o_ref, s_a, s_b, p_a, p_b, acc_scr, m_scr, mb_scr,
               *, n_lat, heads, unroll):
    qs = [q_ref[0, hh] for hh in range(heads)]

    def scores(kt, hh):
        return lax.dot_general(kt, qs[hh], (((1,), (1,)), ((), ())), preferred_element_type=F32)

    for hh in range(heads):
        st = scores(kc_ref[0, hh], hh)
        m = jnp.max(st, axis=0, keepdims=True)
        m_scr[hh] = m
        acc_scr[hh] = jnp.dot(vtc_ref[0, hh, 0], jnp.exp2(st - m).astype(BF16), preferred_element_type=F32)
        st = scores(kl_ref[0, hh, 0], hh)
        s_a[hh] = st
        mb_scr[hh] = jnp.max(st, axis=0, keepdims=True)
        p_b[hh] = jnp.zeros(p_b.shape[1:], p_b.dtype)

    def block_step(j, s_cur, s_nxt, p_wr, p_rd):
        jn = jnp.minimum(j + 1, n_lat - 1)
        jp = jnp.maximum(j - 1, 0)
        for hh in range(heads):
            st_n = scores(kl_ref[0, hh, jn], hh)
            s_nxt[hh] = st_n
            pv = jnp.dot(vtl_ref[0, hh, jp], p_rd[hh], preferred_element_type=F32)
            m = m_scr[hh]
            m_new = jnp.maximum(m, mb_scr[hh])
            p_wr[hh] = jnp.exp2(s_cur[hh] - m_new).astype(BF16)
            acc_scr[hh] = jnp.exp2(m - m_new) * (acc_scr[hh] + pv)
            m_scr[hh] = m_new
            mb_scr[hh] = jnp.max(st_n, axis=0, keepdims=True)

    def trip(t, carry):
        for u in range(0, unroll, 2):
            block_step(unroll * t + u, s_a, s_b, p_a, p_b)
            block_step(unroll * t + u + 1, s_b, s_a, p_b, p_a)
        return carry

    lax.fori_loop(0, n_lat // unroll, trip, 0)
    outs = []
    for hh in range(heads):
        acc = acc_scr[hh] + jnp.dot(vtl_ref[0, hh, n_lat - 1], p_b[hh], preferred_element_type=F32)
        outs.append(acc[:MLA_V] / acc[MLA_V:MLA_V + 1])
    o_ref[0] = jnp.transpose(jnp.concatenate(outs, axis=0)).astype(o_ref.dtype)


def _attention(q, k_ctx, vt_ctx, k_lat, vt_lat, *, tq):
    bsz, nh, seq, _ = q.shape
    heads = LANE // MLA_V
    n_c = k_ctx.shape[2]
    n_lat, _, kb = vt_lat.shape[2:]
    unroll = 4 if n_lat % 4 == 0 else 2
    assert n_lat % unroll == 0
    k_lat = k_lat.reshape(bsz, nh, n_lat, kb, LANE)
    return pl.pallas_call(
        functools.partial(_attn_body, n_lat=n_lat, heads=heads, unroll=unroll),
        grid=(bsz, nh // heads, seq // tq),
        in_specs=[pl.BlockSpec((1, heads, tq, LANE), lambda b, hp, i: (b, hp, i, 0)),
                  pl.BlockSpec((1, heads, n_c, LANE), lambda b, hp, i: (b, hp, 0, 0)),
                  pl.BlockSpec((1, heads, 1, VT_ROWS, n_c), lambda b, hp, i: (b, hp, 0, 0, 0)),
                  pl.BlockSpec((1, heads, n_lat, kb, LANE), lambda b, hp, i: (b, hp, 0, 0, 0)),
                  pl.BlockSpec((1, heads, n_lat, VT_ROWS, kb), lambda b, hp, i: (b, hp, 0, 0, 0))],
        out_specs=pl.BlockSpec((1, tq, LANE), lambda b, hp, i: (b, i, hp)),
        out_shape=jax.ShapeDtypeStruct((bsz, seq, nh * MLA_V), BF16),
        scratch_shapes=[pltpu.VMEM((heads, kb, tq), F32), pltpu.VMEM((heads, kb, tq), F32),
                        pltpu.VMEM((heads, kb, tq), BF16), pltpu.VMEM((heads, kb, tq), BF16),
                        pltpu.VMEM((heads, VT_ROWS, tq), F32), pltpu.VMEM((heads, 1, tq), F32),
                        pltpu.VMEM((heads, 1, tq), F32)],
        compiler_params=_cparams("parallel", "parallel", "arbitrary"),
        name="mla_attention",
    )(q, k_ctx, vt_ctx, k_lat, vt_lat)


def _attn_bounded_body(q_ref, kc_ref, vtc_ref, kl_ref, vtl_ref, shift_ref, o_ref, p_a, p_b, acc_scr,
                       *, n_lat, heads, unroll):
    shift = shift_ref[...].astype(BF16)
    qs = [q_ref[0, hh] + shift for hh in range(heads)]

    def probs(kt, hh):
        st = lax.dot_general(kt, qs[hh], (((1,), (1,)), ((), ())), preferred_element_type=F32)
        return jnp.exp2(st).astype(BF16)

    for hh in range(heads):
        acc_scr[hh] = jnp.dot(vtc_ref[0, hh, 0], probs(kc_ref[0, hh], hh), preferred_element_type=F32)
        p_a[hh] = probs(kl_ref[0, hh, 0], hh)

    def block_step(j, p_cur, p_nxt, produce):
        for hh in range(heads):
            if produce:
                p_nxt[hh] = probs(kl_ref[0, hh, j + 1], hh)
            acc_scr[hh] += jnp.dot(vtl_ref[0, hh, j], p_cur[hh], preferred_element_type=F32)

    def steps(j0, last):
        for u in range(0, unroll, 2):
            block_step(j0 + u, p_a, p_b, True)
            block_step(j0 + u + 1, p_b, p_a, not (last and u + 2 == unroll))

    def trip(t, carry):
        steps(unroll * t, False)
        return carry

    lax.fori_loop(0, n_lat // unroll - 1, trip, 0)
    steps(n_lat - unroll, True)
    outs = [acc_scr[hh][:MLA_V] / acc_scr[hh][MLA_V:MLA_V + 1] for hh in range(heads)]
    o_ref[0] = jnp.transpose(jnp.concatenate(outs, axis=0)).astype(o_ref.dtype)


def _attention_bounded(q, k_ctx, vt_ctx, k_lat, vt_lat, shift, *, tq):
    bsz, nh, seq, _ = q.shape
    heads = LANE // MLA_V
    n_c = k_ctx.shape[2]
    n_lat, _, kb = vt_lat.shape[2:]
    unroll = 4 if n_lat % 4 == 0 else 2
    assert n_lat % unroll == 0
    k_lat = k_lat.reshape(bsz, nh, n_lat, kb, LANE)
    return pl.pallas_call(
        functools.partial(_attn_bounded_body, n_lat=n_lat, heads=heads, unroll=unroll),
        grid=(bsz, nh // heads, seq // tq),
        in_specs=[pl.BlockSpec((1, heads, tq, LANE), lambda b, hp, i: (b, hp, i, 0)),
                  pl.BlockSpec((1, heads, n_c, LANE), lambda b, hp, i: (b, hp, 0, 0)),
                  pl.BlockSpec((1, heads, 1, VT_ROWS, n_c), lambda b, hp, i: (b, hp, 0, 0, 0)),
                  pl.BlockSpec((1, heads, n_lat, kb, LANE), lambda b, hp, i: (b, hp, 0, 0, 0)),
                  pl.BlockSpec((1, heads, n_lat, VT_ROWS, kb), lambda b, hp, i: (b, hp, 0, 0, 0)),
                  pl.BlockSpec((1, LANE), lambda b, hp, i: (0, 0))],
        out_specs=pl.BlockSpec((1, tq, LANE), lambda b, hp, i: (b, i, hp)),
        out_shape=jax.ShapeDtypeStruct((bsz, seq, nh * MLA_V), BF16),
        scratch_shapes=[pltpu.VMEM((heads, kb, tq), BF16), pltpu.VMEM((heads, kb, tq), BF16),
                        pltpu.VMEM((heads, VT_ROWS, tq), F32)],
        compiler_params=_cparams("parallel", "parallel", "arbitrary"),
        name="mla_attention_bounded",
    )(q, k_ctx, vt_ctx, k_lat, vt_lat, shift)


def _pad_heads(w, heads, dim):
    k = w.shape[0]
    w = w.reshape(k, heads, dim)
    return jnp.pad(w, ((0, 0), (0, 0), (0, LANE - dim))).reshape(k, heads * LANE)


def _swap_pairs(a):
    s = a.shape
    return a.reshape(*s[:-1], s[-1] // 2, 2)[..., ::-1].reshape(s)


def _rope_lanes(a, swapped):
    rope = a[..., MLA_NOPE:]
    lo = jnp.zeros_like(a[..., :MLA_NOPE]) if swapped else a[..., :MLA_NOPE]
    mid = _swap_pairs(rope) if swapped else rope
    return jnp.concatenate([lo, mid, jnp.zeros_like(rope)], axis=-1)


def _rope_tables(rows, gain):
    row = jnp.repeat(jnp.arange(rows, dtype=F32), GRID_W)
    col = jnp.tile(jnp.arange(GRID_W, dtype=F32), rows)
    half = MLA_ROPE // 2
    inv_freq = ROPE_BASE ** (-jnp.arange(0, half, 2, dtype=F32) / half)
    ang = jnp.concatenate([row[:, None] * inv_freq, col[:, None] * inv_freq], axis=-1)
    cos = jnp.repeat(jnp.cos(ang), 2, axis=-1)
    sin = jnp.repeat(jnp.sin(ang), 2, axis=-1) * jnp.tile(jnp.array([-1.0, 1.0], F32), MLA_ROPE // 2)
    n = rows * GRID_W
    cos_tab = jnp.concatenate([jnp.ones((n, MLA_NOPE), F32), cos, jnp.zeros((n, MLA_ROPE), F32)], axis=-1)
    sin_tab = jnp.concatenate([jnp.zeros((n, MLA_NOPE), F32), sin, jnp.zeros((n, MLA_ROPE), F32)], axis=-1)
    return cos_tab * _rope_lanes(gain, False), sin_tab * _rope_lanes(gain, True)


def _no_rope_tables(n, gain):
    return jnp.broadcast_to(_rope_lanes(gain, False), (n, LANE)), jnp.zeros((n, LANE), F32)


def _row_tile(seq, want):
    t = min(want, seq)
    assert seq % t == 0 and t % HALO == 0
    return t


def kernel(x, c, ctx, c_ctx, ada_w, ada_b, ab_w_in, ab_gate_bias, ab_conv_w, ab_conv_b, ab_ln_g, ab_ln_b,
           ab_head_gain, ab_w_out, mla_w_in, mla_q_norm, mla_kv_norm, mla_w_uq, mla_w_ukv, mla_q_gain,
           mla_k_gain, mla_w_out, ffn_w_in, ffn_conv_w, ffn_conv_b, ffn_w_out):
    bsz, seq, d = x.shape
    n_ctx = ctx.shape[1]
    assert bsz <= 7 and seq % GRID_W == 0
    tm_l = _row_tile(seq, 512)
    tm_c = _row_tile(n_ctx, 512)
    chunk_l = _row_tile(seq, 256)
    chunk_c = _row_tile(n_ctx, 256)

    c8 = jnp.zeros((8, d), F32).at[:bsz].set(c).at[bsz].set(c_ctx)
    mods = _ada_mods(c8, ada_w, ada_b)

    def mod_vecs(layer):
        parts = jnp.split(mods[layer], 6, axis=-1)
        lat = [p[:bsz, None, :] for p in parts]
        cx = [jnp.broadcast_to(p[bsz][None, None, :], (bsz, 1, d)) for p in parts]
        return lat, cx

    xl = x.reshape(bsz * seq, d)
    xc = ctx.reshape(bsz * n_ctx, d)

    def conv_ffn(xr, mod, layer, seq_len, tm):
        shift, scale, gate = mod[3], mod[4], mod[5]
        return _conv_ffn(xr, 1.0 + scale, shift, gate, ffn_w_in[layer].astype(BF16), ffn_conv_w[layer],
                         ffn_conv_b[layer], ffn_w_out[layer].astype(BF16), seq=seq_len, tm=tm)

    lat, cx = mod_vecs(0)
    w_in = ab_w_in[0]
    wa, wg, wq, wk, wv, wo, wgt = jnp.split(
        w_in, [512, 1024, 1280, 1536, 2048, 2560], axis=1)
    w_main = jnp.concatenate([wa, wg, wv, wo, _pad_heads(wq, MLSTM_HEADS, MLSTM_QK)], axis=1).astype(BF16)
    u_blk, v_blk, o_blk, q_blk = 0, 1, 2, 3
    w_kt = jnp.transpose(_pad_heads(wk * (MLSTM_QK ** -0.5), MLSTM_HEADS, MLSTM_QK)).astype(BF16)
    wgi = jnp.concatenate([wgt[:, 0:4], wgt[:, 8:12]], axis=1)
    wgf = jnp.concatenate([wgt[:, 4:8], wgt[:, 12:16]], axis=1)
    pad8 = lambda a: jnp.pad(a, ((0, 0), (0, LANE - 8)))
    wgate = jnp.concatenate([pad8(wgi), pad8(wgf)], axis=1)
    wgate_hi = wgate.astype(BF16)
    wgate_lo = (wgate - wgate_hi.astype(F32)).astype(BF16)
    w_gate = jnp.stack([wgate_hi, wgate_lo])
    gb = ab_gate_bias[0]
    gbias = jnp.concatenate([pad8(jnp.concatenate([gb[0:4], gb[8:12]])[None, :]),
                             pad8(jnp.concatenate([gb[4:8], gb[12:16]])[None, :])], axis=1)
    w_out0 = ab_w_out[0].astype(BF16)

    def mixer0(xr, mod, seq_len, tm, chunk, s0, m0):
        y, kt, gates = _ab_in(xr, 1.0 + mod[1], mod[0], w_main, w_kt, w_gate, bsz=bsz, seq=seq_len, tm=tm)
        gc, gr = _gate_prep(gates, gbias, bsz=bsz, seq=seq_len, chunk=chunk)
        hf, hb, s1, m1 = _mlstm(y, kt, gc, gr, s0, m0, bsz=bsz, seq=seq_len, chunk=chunk,
                                q_blk=q_blk, v_blk=v_blk)
        out = _merge(y, hf.reshape(-1, hf.shape[-1]), hb.reshape(-1, hb.shape[-1]), ab_conv_w[0], ab_conv_b[0],
                     ab_ln_g[0], ab_ln_b[0], ab_head_gain[0], xr, mod[2], w_out0, seq=seq_len, tm=tm,
                     u_blk=u_blk, o_blk=o_blk)
        return out, s1, m1

    nch = 2 * MLSTM_HEADS
    s_zero = jnp.zeros((bsz, nch, LANE, 2 * LANE), F32)
    m_zero = jnp.zeros((bsz, nch, 8, LANE), F32)
    xc, s_ctx, m_ctx = mixer0(xc, cx, n_ctx, tm_c, chunk_c, s_zero, m_zero)
    xl, _, _ = mixer0(xl, lat, seq, tm_l, chunk_l, s_ctx, m_ctx)
    xl = conv_ffn(xl, lat, 0, seq, tm_l)
    xc = conv_ffn(xc, cx, 0, n_ctx, tm_c)

    lat, cx = mod_vecs(1)
    w_in = mla_w_in[0]
    w_kr = w_in[:, MLA_Q_LORA + MLA_KV_LORA:]
    zpad = lambda n: jnp.zeros((d, n), F32)
    w_in1 = jnp.concatenate([w_in[:, :MLA_Q_LORA + MLA_KV_LORA],
                             zpad(MLA_NOPE), w_kr, zpad(MLA_ROPE),
                             zpad(MLA_NOPE), _swap_pairs(w_kr), zpad(MLA_ROPE)], axis=1).astype(BF16)
    w_uq = mla_w_uq[0].reshape(MLA_Q_LORA, MLA_HEADS, MLA_QK)
    w_uq_a = _rope_lanes(w_uq, False).reshape(MLA_Q_LORA, MLA_HEADS * LANE).astype(BF16)
    w_uq_b = _rope_lanes(w_uq, True).reshape(MLA_Q_LORA, MLA_HEADS * LANE).astype(BF16)
    w_ukv = mla_w_ukv[0].reshape(MLA_KV_LORA, MLA_HEADS, MLA_NOPE + MLA_V)
    w_uk = _pad_heads(w_ukv[..., :MLA_NOPE].reshape(MLA_KV_LORA, -1), MLA_HEADS, MLA_NOPE).astype(BF16)
    w_uvt = jnp.transpose(w_ukv[..., MLA_NOPE:].reshape(MLA_KV_LORA, -1)).astype(BF16)
    ind = (jnp.arange(MLA_HEADS * LANE)[:, None] // LANE == jnp.arange(LANE)[None, :]).astype(BF16)
    q_tabs = _rope_tables(seq // GRID_W, mla_q_gain[0] * (MLA_QK ** -0.5 * math.log2(math.e)))
    k_tabs = _rope_tables(seq // GRID_W, mla_k_gain[0])
    kc_tabs = _no_rope_tables(n_ctx, mla_k_gain[0])

    tk = 256
    kv_blk = min(1024, seq // 2)
    assert seq % kv_blk == 0 and kv_blk % tk == 0 and n_ctx % tk == 0
    c_ctx_ = _mm(xc, w_in1, tm=tm_c, out_dtype=F32, group_tiles=n_ctx // tm_c,
                 pro=(1.0 + cx[1], cx[0]), name="mla_in")
    q, k_lat, vt_lat = _mla_prep(xl, 1.0 + lat[1], lat[0], w_in1, mla_q_norm[0], mla_kv_norm[0], w_uq_a, w_uq_b,
                                 w_uk, w_uvt, ind, q_tabs, k_tabs, bsz=bsz, seq=seq, tm=tk, vt_width=kv_blk)
    k_ctx, vt_ctx = _kv_prep(c_ctx_, mla_kv_norm[0], w_uk, w_uvt, ind, *kc_tabs, bsz=bsz, seq=n_ctx, tm=tk,
                             vt_width=n_ctx)
    bound = (1.02 * MLA_QK ** 0.5 * math.log2(math.e)) * jnp.max(jnp.abs(mla_q_gain[0])) * jnp.max(
        jnp.abs(mla_k_gain[0]))
    shift = jnp.zeros((1, LANE), F32).at[0, MLA_QK].set(-bound)
    att = lax.cond(
        2.0 * bound < MAX_EXP2_SPAN,
        lambda: _attention_bounded(q, k_ctx, vt_ctx, k_lat, vt_lat, shift, tq=min(2048, seq)),
        lambda: _attention(q, k_ctx, vt_ctx, k_lat, vt_lat, tq=min(512, seq)))
    xl = _mm(att.reshape(bsz * seq, -1), mla_w_out[0].astype(BF16), tm=tm_l, out_dtype=F32,
             group_tiles=seq // tm_l, res=(xl, lat[2]), name="mla_out")
    xl = conv_ffn(xl, lat, 1, seq, tm_l)
    return xl.reshape(bsz, seq, d)
```

```python
import functools
import math

import jax
import jax.numpy as jnp
from jax import lax
from jax.experimental import pallas as pl
from jax.experimental.pallas import tpu as pltpu

F32 = jnp.float32
BF16 = jnp.bfloat16

EPS = 1e-6
GRID_W = 64
CONV_DIM = 512
CONV_WIDTH = 31
MLSTM_HEADS = 4
MLSTM_QK = 64
MLSTM_V = 128
MLA_HEADS = 16
MLA_Q_LORA = 512
MLA_KV_LORA = 256
MLA_NOPE = 64
MLA_ROPE = 32
MLA_V = 64
MLA_QK = MLA_NOPE + MLA_ROPE
ROPE_BASE = 10000.0
FFN_DIM = 2816

LANE = 128
SUBLANE = 8
HALO = 16
CONV_ROWS = 64
MAX_EXP2_SPAN = 100.0
VT_ROWS = MLA_V + 16
VMEM_LIMIT = 48 * 1024 * 1024


def _cparams(*sem):
    return pltpu.CompilerParams(dimension_semantics=sem, vmem_limit_bytes=VMEM_LIMIT)


def _const_spec(shape):
    nd = len(shape)
    return pl.BlockSpec(shape, lambda *_: (0,) * nd, pipeline_mode=pl.Buffered(1))


def _split3(a):
    hi = a.astype(BF16)
    r = a - hi.astype(F32)
    mid = r.astype(BF16)
    lo = (r - mid.astype(F32)).astype(BF16)
    return hi, mid, lo


def _rms_rows(xf):
    return xf * lax.rsqrt(jnp.mean(xf * xf, axis=-1, keepdims=True) + EPS)


def _ada_body(c_ref, w_ref, b_ref, o_ref):
    c = c_ref[...]
    s = c * jax.nn.sigmoid(c)
    s_hi, s_lo, _ = _split3(s)
    w = w_ref[0]
    w_hi = w.astype(BF16)
    w_lo = (w - w_hi.astype(F32)).astype(BF16)
    acc = jnp.dot(s_hi, w_hi, preferred_element_type=F32)
    acc += jnp.dot(s_hi, w_lo, preferred_element_type=F32)
    acc += jnp.dot(s_lo, w_hi, preferred_element_type=F32)
    o_ref[0] = acc + b_ref[0]


def _ada_mods(c8, ada_w, ada_b):
    depth, d, n = ada_w.shape
    tn = 1536
    return pl.pallas_call(
        _ada_body,
        grid=(depth, n // tn),
        in_specs=[pl.BlockSpec((8, d), lambda l, j: (0, 0)),
                  pl.BlockSpec((1, d, tn), lambda l, j: (l, 0, j)),
                  pl.BlockSpec((1, 1, tn), lambda l, j: (l, 0, j))],
        out_specs=pl.BlockSpec((1, 8, tn), lambda l, j: (l, 0, j)),
        out_shape=jax.ShapeDtypeStruct((depth, 8, n), F32),
        compiler_params=_cparams("parallel", "parallel"),
        name="ada_mods",
    )(c8, ada_w, ada_b.reshape(depth, 1, n))


def _mm_body(*refs, has_pro, has_res, n_chunk):
    it = iter(refs)
    x_ref = next(it)
    if has_pro:
        a_ref, b_ref = next(it), next(it)
    w_ref = next(it)
    if has_res:
        r_ref, g_ref = next(it), next(it)
    o_ref = next(it)
    if has_pro:
        h = (_rms_rows(x_ref[...].astype(F32)) * a_ref[0] + b_ref[0]).astype(BF16)
    else:
        h = x_ref[...].astype(BF16)
    n = o_ref.shape[-1]
    for c0 in range(0, n, n_chunk):
        c1 = min(c0 + n_chunk, n)
        acc = jnp.dot(h, w_ref[:, c0:c1], preferred_element_type=F32)
        if has_res:
            acc = r_ref[:, c0:c1] + g_ref[0][:, c0:c1] * acc
        o_ref[:, c0:c1] = acc.astype(o_ref.dtype)


def _mm(x, w, *, tm, out_dtype, group_tiles, pro=None, res=None, x_cols=None, n_chunk=512, name):
    m = x.shape[0]
    k, n = w.shape
    xcol = 0 if x_cols is None else x_cols
    in_specs = [pl.BlockSpec((tm, k), lambda i: (i, xcol))]
    args = [x]
    if pro is not None:
        in_specs += [pl.BlockSpec((1, 1, k), lambda i: (i // group_tiles, 0, 0))] * 2
        args += list(pro)
    in_specs.append(_const_spec((k, n)))
    args.append(w)
    if res is not None:
        in_specs += [pl.BlockSpec((tm, n), lambda i: (i, 0)),
                     pl.BlockSpec((1, 1, n), lambda i: (i // group_tiles, 0, 0))]
        args += list(res)
    return pl.pallas_call(
        functools.partial(_mm_body, has_pro=pro is not None, has_res=res is not None, n_chunk=n_chunk),
        grid=(m // tm,),
        in_specs=in_specs,
        out_specs=pl.BlockSpec((tm, n), lambda i: (i, 0)),
        out_shape=jax.ShapeDtypeStruct((m, n), out_dtype),
        compiler_params=_cparams("parallel"),
        name=name,
    )(*args)


def _ab_in_body(x_ref, a_ref, b_ref, w_ref, wkt_ref, wg_ref, y_ref, kt_ref, g_ref, *, n_chunk):
    hf = _rms_rows(x_ref[...]) * a_ref[0] + b_ref[0]
    h = hf.astype(BF16)
    a = jnp.dot(h, w_ref[:, :CONV_DIM], preferred_element_type=F32)
    g = jnp.dot(h, w_ref[:, CONV_DIM:2 * CONV_DIM], preferred_element_type=F32)
    y_ref[:, :CONV_DIM] = (a * jax.nn.sigmoid(g)).astype(y_ref.dtype)
    n = w_ref.shape[-1]
    for c0 in range(2 * CONV_DIM, n, n_chunk):
        c1 = min(c0 + n_chunk, n)
        y_ref[:, c0 - CONV_DIM:c1 - CONV_DIM] = jnp.dot(
            h, w_ref[:, c0:c1], preferred_element_type=F32).astype(y_ref.dtype)
    kt_ref[0] = lax.dot_general(wkt_ref[...], h, (((1,), (1,)), ((), ())),
                                preferred_element_type=F32).astype(kt_ref.dtype)
    h_hi, h_lo, _ = _split3(hf)
    acc = jnp.dot(h_hi, wg_ref[0], preferred_element_type=F32)
    acc += jnp.dot(h_hi, wg_ref[1], preferred_element_type=F32)
    acc += jnp.dot(h_lo, wg_ref[0], preferred_element_type=F32)
    g_ref[...] = acc


def _ab_in(x, mod_a, mod_b, w_main, w_kt, w_gate, *, bsz, seq, tm):
    m, d = x.shape
    n = w_main.shape[1] - CONV_DIM
    tiles = seq // tm
    return pl.pallas_call(
        functools.partial(_ab_in_body, n_chunk=512),
        grid=(m // tm,),
        in_specs=[pl.BlockSpec((tm, d), lambda i: (i, 0)),
                  pl.BlockSpec((1, 1, d), lambda i: (i // tiles, 0, 0)),
                  pl.BlockSpec((1, 1, d), lambda i: (i // tiles, 0, 0)),
                  _const_spec(w_main.shape), _const_spec(w_kt.shape), _const_spec(w_gate.shape)],
        out_specs=[pl.BlockSpec((tm, n), lambda i: (i, 0)),
                   pl.BlockSpec((1, w_kt.shape[0], tm), lambda i: (i // tiles, 0, i % tiles)),
                   pl.BlockSpec((tm, 2 * LANE), lambda i: (i, 0))],
        out_shape=[jax.ShapeDtypeStruct((m, n), BF16),
                   jax.ShapeDtypeStruct((bsz, w_kt.shape[0], seq), BF16),
                   jax.ShapeDtypeStruct((m, 2 * LANE), F32)],
        compiler_params=_cparams("parallel"),
        name="ab_in",
    )(x, mod_a, mod_b, w_main, w_kt, w_gate)


def _gate_prep_body(g_ref, bias_ref, gc_ref, gr_ref, *, chunk):
    g = g_ref[0]
    li = g[:, :LANE] + bias_ref[:, :LANE]
    lf = jax.nn.log_sigmoid(g[:, LANE:] + bias_ref[:, LANE:])
    row = lax.broadcasted_iota(jnp.int32, (chunk, chunk), 0)
    col = lax.broadcasted_iota(jnp.int32, (chunk, chunk), 1)
    lower = (col <= row).astype(BF16)
    upper = (col >= row).astype(BF16)
    b_f = jnp.zeros((chunk, LANE), F32)
    b_b = jnp.zeros((chunk, LANE), F32)
    for piece in _split3(lf):
        b_f += jnp.dot(lower, piece, preferred_element_type=F32)
        b_b += jnp.dot(upper, piece, preferred_element_type=F32)
    lane = lax.broadcasted_iota(jnp.int32, (chunk, LANE), 1)
    fwd = lane < MLSTM_HEADS
    b = jnp.where(fwd, b_f, b_b)
    r = li - b
    tok = lax.broadcasted_iota(jnp.int32, (chunk, LANE), 0)
    rm_f, rm_b = r, r
    step = 1
    while step < chunk:
        rm_f = jnp.maximum(rm_f, jnp.where(tok >= step, pltpu.roll(rm_f, step, axis=0), -jnp.inf))
        rm_b = jnp.maximum(rm_b, jnp.where(tok < chunk - step, pltpu.roll(rm_b, chunk - step, axis=0), -jnp.inf))
        step *= 2
    gc_ref[0, :, :LANE] = b
    gc_ref[0, :, LANE:] = jnp.where(fwd, rm_f, rm_b)
    gr_ref[0] = jnp.transpose(r)[:8, :]


def _gate_prep(gates, bias, *, bsz, seq, chunk):
    g3 = gates.reshape(bsz, seq, 2 * LANE)
    nc = seq // chunk
    return pl.pallas_call(
        functools.partial(_gate_prep_body, chunk=chunk),
        grid=(bsz, nc),
        in_specs=[pl.BlockSpec((1, chunk, 2 * LANE), lambda b, c: (b, c, 0)),
                  pl.BlockSpec((1, 2 * LANE), lambda b, c: (0, 0))],
        out_specs=[pl.BlockSpec((1, chunk, 2 * LANE), lambda b, c: (b, c, 0)),
                   pl.BlockSpec((1, 8, chunk), lambda b, c: (b, 0, c))],
        out_shape=[jax.ShapeDtypeStruct((bsz, seq, 2 * LANE), F32),
                   jax.ShapeDtypeStruct((bsz, 8, seq), F32)],
        compiler_params=_cparams("parallel", "parallel"),
        name="gate_prep",
    )(g3, bias)


def _mlstm_body(qf_ref, vf_ref, ktf_ref, gcf_ref, grf_ref,
                qb_ref, vb_ref, ktb_ref, gcb_ref, grb_ref,
                s0_ref, m0_ref,
                hf_ref, hb_ref, s1_ref, m1_ref,
                s_scr, m_scr, *, chunk):
    c = pl.program_id(1)
    nc = pl.num_programs(1)

    @pl.when(c == 0)
    def _():
        s_scr[...] = s0_ref[0]
        m_scr[...] = m0_ref[0]

    row = lax.broadcasted_iota(jnp.int32, (chunk, chunk), 0)
    col = lax.broadcasted_iota(jnp.int32, (chunk, chunk), 1)
    ones_col = (lax.broadcasted_iota(jnp.int32, (chunk, LANE), 1) == 0).astype(BF16)

    for d in range(2):
        q_ref, v_ref, kt_ref, gc_ref, gr_ref, h_ref = (
            (qf_ref, vf_ref, ktf_ref, gcf_ref, grf_ref, hf_ref) if d == 0 else
            (qb_ref, vb_ref, ktb_ref, gcb_ref, grb_ref, hb_ref))
        mask = (col <= row) if d == 0 else (col >= row)
        for hd in range(MLSTM_HEADS):
            ch = d * MLSTM_HEADS + hd
            sl = slice(hd * LANE, (hd + 1) * LANE)
            q = q_ref[0][:, sl]
            kt = kt_ref[0][sl, :]
            v = v_ref[0][:, sl]
            bcol = gc_ref[0][:, ch:ch + 1]
            rmcol = gc_ref[0][:, LANE + ch:LANE + ch + 1]
            rrow = gr_ref[0][ch:ch + 1, :]
            m_old = m_scr[ch][0:1, 0:1]
            b_last = bcol[chunk - 1:chunk, :] if d == 0 else bcol[0:1, :]
            cm = jnp.maximum(rmcol, m_old)
            s = jnp.dot(q, kt, preferred_element_type=F32) * jnp.exp(jnp.where(mask, rrow - cm, -jnp.inf))
            inter = jnp.exp(m_old - cm)
            v_ext = jnp.concatenate([v, ones_col], axis=1)
            st = s_scr[ch]
            nd = (jnp.dot(s.astype(BF16), v_ext, preferred_element_type=F32)
                  + inter * jnp.dot(q, st.astype(BF16), preferred_element_type=F32))
            num = nd[:, :LANE]
            den = nd[:, LANE:LANE + 1]
            h_ref[0, :, sl] = num * (1.0 / jnp.maximum(jnp.abs(den), jnp.exp(-(bcol + cm))))
            log_w = b_last + rrow
            m_new = jnp.maximum(b_last + m_old, jnp.max(log_w, axis=1, keepdims=True))
            w = jnp.exp(log_w - m_new)
            decay = jnp.exp(b_last + m_old - m_new)
            kw = (kt.astype(F32) * w).astype(BF16)
            s_scr[ch] = decay * st + jnp.dot(kw, v_ext, preferred_element_type=F32)
            m_scr[ch] = jnp.broadcast_to(m_new, (8, LANE))

    @pl.when(c == nc - 1)
    def _():
        s1_ref[0] = s_scr[...]
        m1_ref[0] = m_scr[...]


def _mlstm(y, kt, gc, gr, s0, m0, *, bsz, seq, chunk, q_blk, v_blk):
    nc = seq // chunk
    width = MLSTM_HEADS * LANE
    y3 = y.reshape(bsz, seq, y.shape[-1])
    nch = 2 * MLSTM_HEADS

    def fwd(b, c):
        return c

    def bwd(b, c):
        return nc - 1 - c

    def specs(pos):
        return [pl.BlockSpec((1, chunk, width), lambda b, c: (b, pos(b, c), q_blk)),
                pl.BlockSpec((1, chunk, width), lambda b, c: (b, pos(b, c), v_blk)),
                pl.BlockSpec((1, width, chunk), lambda b, c: (b, 0, pos(b, c))),
                pl.BlockSpec((1, chunk, 2 * LANE), lambda b, c: (b, pos(b, c), 0)),
                pl.BlockSpec((1, 8, chunk), lambda b, c: (b, 0, pos(b, c)))]

    state_specs = [pl.BlockSpec((1, nch, LANE, 2 * LANE), lambda b, c: (b, 0, 0, 0)),
                   pl.BlockSpec((1, nch, 8, LANE), lambda b, c: (b, 0, 0, 0))]
    return pl.pallas_call(
        functools.partial(_mlstm_body, chunk=chunk),
        grid=(bsz, nc),
        in_specs=specs(fwd) + specs(bwd) + state_specs,
        out_specs=[pl.BlockSpec((1, chunk, width), lambda b, c: (b, c, 0)),
                   pl.BlockSpec((1, chunk, width), lambda b, c: (b, nc - 1 - c, 0))] + state_specs,
        out_shape=[jax.ShapeDtypeStruct((bsz, seq, width), F32),
                   jax.ShapeDtypeStruct((bsz, seq, width), F32),
                   jax.ShapeDtypeStruct(s0.shape, F32),
                   jax.ShapeDtypeStruct(m0.shape, F32)],
        scratch_shapes=[pltpu.VMEM((nch, LANE, 2 * LANE), F32), pltpu.VMEM((nch, 8, LANE), F32)],
        compiler_params=_cparams("parallel", "arbitrary"),
        name="mlstm",
    )(y3, y3, kt, gc, gr, y3, y3, kt, gc, gr, s0, m0)


def _merge_body(uc_ref, up_ref, un_ref, o_ref, hf_ref, hb_ref,
                cw_ref, cb_ref, lg_ref, lb_ref, hg_ref, x_ref, gate_ref, wout_ref, out_ref,
                z_ref, scr, sh_scr, *, tm, tiles):
    i = pl.program_id(0)
    not_first = (i % tiles != 0).astype(F32)
    not_last = (i % tiles != tiles - 1).astype(F32)
    scr[0:HALO, :] = up_ref[...].astype(F32) * not_first
    scr[HALO:HALO + tm, :] = uc_ref[...].astype(F32)
    scr[HALO + tm:2 * HALO + tm, :] = un_ref[...].astype(F32) * not_last
    rows = tm + 2 * HALO
    for r in range(SUBLANE):
        sh_scr[r, 0:rows - SUBLANE, :] = scr[r:rows - SUBLANE + r, :]
    first = HALO - (CONV_WIDTH - 1) // 2

    def row_block(rb, carry):
        base = pl.multiple_of(rb * CONV_ROWS, CONV_ROWS)
        acc = jnp.zeros((CONV_ROWS, CONV_DIM), F32)
        for k in range(CONV_WIDTH):
            off = first + k
            acc += sh_scr[off % SUBLANE, pl.ds(base + off - off % SUBLANE, CONV_ROWS), :] * cw_ref[k:k + 1, :]
        acc += cb_ref[...]
        mu = jnp.mean(acc, axis=-1, keepdims=True)
        cen = acc - mu
        var = jnp.mean(cen * cen, axis=-1, keepdims=True)
        u = cen * lax.rsqrt(var + EPS) * lg_ref[...] + lb_ref[...]
        z_ref[pl.ds(base, CONV_ROWS), :CONV_DIM] = (u * jax.nn.sigmoid(u)).astype(z_ref.dtype)
        return carry

    lax.fori_loop(0, tm // CONV_ROWS, row_block, 0)
    hs = hf_ref[...] + hb_ref[...]
    og = hg_ref[...] * jax.nn.sigmoid(o_ref[...].astype(F32))
    for hd in range(MLSTM_HEADS):
        sl = slice(hd * MLSTM_V, (hd + 1) * MLSTM_V)
        z_ref[:, CONV_DIM + hd * MLSTM_V:CONV_DIM + (hd + 1) * MLSTM_V] = (
            _rms_rows(hs[:, sl]) * og[:, sl]).astype(z_ref.dtype)
    out_ref[...] = x_ref[...] + gate_ref[0] * jnp.dot(z_ref[...], wout_ref[...], preferred_element_type=F32)


def _merge(y, hf, hb, conv_w, conv_b, ln_g, ln_b, head_gain, x, gate, w_out, *, seq, tm, u_blk, o_blk):
    m = y.shape[0]
    d = x.shape[1]
    assert tm % CONV_ROWS == 0
    tiles = seq // tm
    r = tm // HALO
    last = m // HALO - 1
    cur = lambda blk: pl.BlockSpec((tm, CONV_DIM), lambda i: (i, blk))
    prev = lambda blk: pl.BlockSpec((HALO, CONV_DIM), lambda i: (jnp.maximum(i * r - 1, 0), blk))
    nxt = lambda blk: pl.BlockSpec((HALO, CONV_DIM), lambda i: (jnp.minimum((i + 1) * r, last), blk))
    vec = lambda a: a.reshape(1, -1)
    hspec = pl.BlockSpec((tm, CONV_DIM), lambda i: (i, 0))
    return pl.pallas_call(
        functools.partial(_merge_body, tm=tm, tiles=tiles),
        grid=(m // tm,),
        in_specs=[cur(u_blk), prev(u_blk), nxt(u_blk), cur(o_blk),
                  hspec, hspec,
                  _const_spec((CONV_WIDTH, CONV_DIM)), _const_spec((1, CONV_DIM)), _const_spec((1, CONV_DIM)),
                  _const_spec((1, CONV_DIM)), _const_spec((1, CONV_DIM)),
                  pl.BlockSpec((tm, d), lambda i: (i, 0)), pl.BlockSpec((1, 1, d), lambda i: (i // tiles, 0, 0)),
                  _const_spec(w_out.shape)],
        out_specs=pl.BlockSpec((tm, d), lambda i: (i, 0)),
        out_shape=jax.ShapeDtypeStruct((m, d), F32),
        scratch_shapes=[pltpu.VMEM((tm, 2 * CONV_DIM), BF16),
                        pltpu.VMEM((tm + 2 * HALO, CONV_DIM), F32),
                        pltpu.VMEM((SUBLANE, tm + 2 * HALO, CONV_DIM), F32)],
        compiler_params=_cparams("parallel"),
        name="ab_merge_out",
    )(y, y, y, y, hf, hb, conv_w, vec(conv_b), vec(ln_g), vec(ln_b), vec(head_gain), x, gate, w_out)


def _ffn_body(xc_ref, xp_ref, xn_ref, a_ref, b_ref, gate_ref, win_ref, cw_ref, cb_ref, wout_ref, *rest,
              tm, tiles, f_chunk, pre):
    if pre:
        yc_ref, yp_ref, yn_ref, wpre_ref, gpre_ref, o_ref, h_scr, u_scr, y_scr, x_scr = rest
        y_scr[0:HALO, :] = yp_ref[...]
        y_scr[HALO:HALO + tm, :] = yc_ref[...]
        y_scr[HALO + tm:, :] = yn_ref[...]
        proj = gpre_ref[0] * jnp.dot(y_scr[...], wpre_ref[...], preferred_element_type=F32)
        x_scr[0:HALO, :] = xp_ref[...] + proj[0:HALO]
        x_scr[HALO:HALO + tm, :] = xc_ref[...] + proj[HALO:HALO + tm]
        x_scr[HALO + tm:, :] = xn_ref[...] + proj[HALO + tm:]
        xp_ref, xc_ref, xn_ref = x_scr.at[0:HALO], x_scr.at[HALO:HALO + tm], x_scr.at[HALO + tm:]
    else:
        o_ref, h_scr, u_scr = rest
    i = pl.program_id(0)
    not_first = (i % tiles != 0).astype(F32)
    not_last = (i % tiles != tiles - 1).astype(F32)

    def modulate(x):
        return _rms_rows(x) * a_ref[0] + b_ref[0]

    h_scr[0:HALO, :] = (modulate(xp_ref[...]) * not_first).astype(BF16)
    h_scr[HALO:HALO + tm, :] = modulate(xc_ref[...]).astype(BF16)
    h_scr[HALO + tm:, :] = (modulate(xn_ref[...]) * not_last).astype(BF16)
    f = cw_ref.shape[1]
    for c0 in range(0, f, f_chunk):
        c1 = min(c0 + f_chunk, f)
        g = jnp.dot(h_scr[...], win_ref[:, c0:c1], preferred_element_type=F32)
        v = jnp.dot(h_scr[HALO:HALO + tm, :], win_ref[:, f + c0:f + c1], preferred_element_type=F32)
        y = (g[HALO - 1:HALO - 1 + tm] * cw_ref[0:1, c0:c1] + g[HALO:HALO + tm] * cw_ref[1:2, c0:c1]
             + g[HALO + 1:HALO + 1 + tm] * cw_ref[2:3, c0:c1] + cb_ref[:, c0:c1])
        u_scr[:, c0:c1] = (jax.nn.gelu(y, approximate=True) * v).astype(BF16)
    o_ref[...] = xc_ref[...] + gate_ref[0] * jnp.dot(u_scr[...], wout_ref[...], preferred_element_type=F32)


def _conv_ffn(x, mod_a, mod_b, gate, w_in, conv_w, conv_b, w_out, *, seq, tm, pre=None):
    m, d = x.shape
    f = conv_w.shape[1]
    tiles = seq // tm
    r = tm // HALO
    last = m // HALO - 1
    grp = lambda i: (i // tiles, 0, 0)
    cur = lambda w: pl.BlockSpec((tm, w), lambda i: (i, 0))
    prev = lambda w: pl.BlockSpec((HALO, w), lambda i: (jnp.maximum(i * r - 1, 0), 0))
    nxt = lambda w: pl.BlockSpec((HALO, w), lambda i: (jnp.minimum((i + 1) * r, last), 0))
    in_specs = [cur(d), prev(d), nxt(d),
                pl.BlockSpec((1, 1, d), grp), pl.BlockSpec((1, 1, d), grp), pl.BlockSpec((1, 1, d), grp),
                _const_spec(w_in.shape), _const_spec((3, f)), _const_spec((1, f)), _const_spec(w_out.shape)]
    args = [x, x, x, mod_a, mod_b, gate, w_in, conv_w, conv_b.reshape(1, f), w_out]
    scratch = [pltpu.VMEM((tm + 2 * HALO, d), BF16), pltpu.VMEM((tm, f), BF16)]
    if pre is not None:
        y, w_pre, gate_pre = pre
        ky = y.shape[1]
        in_specs += [cur(ky), prev(ky), nxt(ky), _const_spec(w_pre.shape), pl.BlockSpec((1, 1, d), grp)]
        args += [y, y, y, w_pre, gate_pre]
        scratch += [pltpu.VMEM((tm + 2 * HALO, ky), BF16), pltpu.VMEM((tm + 2 * HALO, d), F32)]
    return pl.pallas_call(
        functools.partial(_ffn_body, tm=tm, tiles=tiles, f_chunk=256, pre=pre is not None),
        grid=(m // tm,),
        in_specs=in_specs,
        out_specs=pl.BlockSpec((tm, d), lambda i: (i, 0)),
        out_shape=jax.ShapeDtypeStruct((m, d), F32),
        scratch_shapes=scratch,
        compiler_params=_cparams("parallel"),
        name="conv_ffn" if pre is None else "proj_conv_ffn",
    )(*args)


def _head_rsqrt(xa, ind_ref):
    sq = (xa * xa).astype(BF16)
    ms = jnp.dot(sq, ind_ref[...], preferred_element_type=F32) * (1.0 / MLA_QK)
    return lax.rsqrt(ms + EPS)


def _q_heads(cq, qn_ref, wa_ref, wb_ref, ind_ref, gc_ref, gs_ref, q_ref):
    h = (_rms_rows(cq) * qn_ref[...]).astype(BF16)
    xa = jnp.dot(h, wa_ref[...], preferred_element_type=F32)
    xb = jnp.dot(h, wb_ref[...], preferred_element_type=F32)
    rs = _head_rsqrt(xa, ind_ref)
    gc, gs = gc_ref[...], gs_ref[...]
    for hd in range(MLA_HEADS):
        sl = slice(hd * LANE, (hd + 1) * LANE)
        q_ref[0, hd] = (rs[:, hd:hd + 1] * (xa[:, sl] * gc + xb[:, sl] * gs)).astype(q_ref.dtype)


def _kv_heads(ckv, kra, krb, kn_ref, wk_ref, wvt_ref, ind_ref, gc_ref, gs_ref, k_ref, vt_ref):
    h = (_rms_rows(ckv) * kn_ref[...]).astype(BF16)
    gc = gc_ref[...]
    xa = jnp.dot(h, wk_ref[...], preferred_element_type=F32)
    xa = xa + jnp.concatenate([kra] * MLA_HEADS, axis=1)
    rs = _head_rsqrt(xa, ind_ref)
    rot = krb * gs_ref[...]
    one96 = (lax.broadcasted_iota(jnp.int32, (1, LANE), 1) == MLA_QK).astype(F32)
    for hd in range(MLA_HEADS):
        sl = slice(hd * LANE, (hd + 1) * LANE)
        k_ref[0, hd] = (rs[:, hd:hd + 1] * (xa[:, sl] * gc + rot) + one96).astype(k_ref.dtype)
    vt = lax.dot_general(wvt_ref[...], h, (((1,), (1,)), ((), ())), preferred_element_type=F32)
    tm = vt.shape[1]
    ones_rows = (lax.broadcasted_iota(jnp.int32, (VT_ROWS - MLA_V, tm), 0) == 0).astype(vt_ref.dtype)
    for hd in range(MLA_HEADS):
        vt_ref[0, hd, 0, :MLA_V, :] = vt[hd * MLA_V:(hd + 1) * MLA_V, :].astype(vt_ref.dtype)
        vt_ref[0, hd, 0, MLA_V:, :] = ones_rows


def _mla_prep_body(x_ref, a_ref, b_ref, win_ref, qn_ref, kn_ref, wa_ref, wb_ref, wk_ref, wvt_ref, ind_ref,
                   qgc_ref, qgs_ref, kgc_ref, kgs_ref, q_ref, k_ref, vt_ref):
    h = (_rms_rows(x_ref[...]) * a_ref[0] + b_ref[0]).astype(BF16)
    c = jnp.dot(h, win_ref[...], preferred_element_type=F32)
    kv0 = MLA_Q_LORA
    kr0 = MLA_Q_LORA + MLA_KV_LORA
    _q_heads(c[:, :kv0], qn_ref, wa_ref, wb_ref, ind_ref, qgc_ref, qgs_ref, q_ref)
    _kv_heads(c[:, kv0:kr0], c[:, kr0:kr0 + LANE], c[:, kr0 + LANE:], kn_ref, wk_ref, wvt_ref, ind_ref,
              kgc_ref, kgs_ref, k_ref, vt_ref)


def _mla_prep(x, mod_a, mod_b, w_in, q_norm, kv_norm, w_a, w_b, w_uk, w_uvt, ind, q_tabs, k_tabs,
              *, bsz, seq, tm, vt_width):
    d = x.shape[1]
    tiles = seq // tm
    sub = vt_width // tm
    tab = pl.BlockSpec((tm, LANE), lambda i: (i % tiles, 0))
    grp = pl.BlockSpec((1, 1, d), lambda i: (i // tiles, 0, 0))
    head_major = pl.BlockSpec((1, MLA_HEADS, tm, LANE), lambda i: (i // tiles, 0, i % tiles, 0))
    return pl.pallas_call(
        _mla_prep_body,
        grid=(bsz * tiles,),
        in_specs=[pl.BlockSpec((tm, d), lambda i: (i, 0)), grp, grp, _const_spec(w_in.shape),
                  _const_spec((1, MLA_Q_LORA)), _const_spec((1, MLA_KV_LORA)),
                  _const_spec(w_a.shape), _const_spec(w_b.shape), _const_spec(w_uk.shape), _const_spec(w_uvt.shape),
                  _const_spec(ind.shape), tab, tab, tab, tab],
        out_specs=[head_major, head_major,
                   pl.BlockSpec((1, MLA_HEADS, 1, VT_ROWS, tm),
                                lambda i: (i // tiles, 0, (i % tiles) // sub, 0, (i % tiles) % sub))],
        out_shape=[jax.ShapeDtypeStruct((bsz, MLA_HEADS, seq, LANE), BF16),
                   jax.ShapeDtypeStruct((bsz, MLA_HEADS, seq, LANE), BF16),
                   jax.ShapeDtypeStruct((bsz, MLA_HEADS, seq // vt_width, VT_ROWS, vt_width), BF16)],
        compiler_params=_cparams("parallel"),
        name="mla_prep",
    )(x, mod_a, mod_b, w_in, q_norm.reshape(1, -1), kv_norm.reshape(1, -1), w_a, w_b, w_uk, w_uvt, ind,
      *q_tabs, *k_tabs)


def _kv_prep_body(ckv_ref, kra_ref, krb_ref, kn_ref, wk_ref, wvt_ref, ind_ref, gc_ref, gs_ref,
                  k_ref, vt_ref):
    _kv_heads(ckv_ref[...], kra_ref[...], krb_ref[...], kn_ref, wk_ref, wvt_ref, ind_ref,
              gc_ref, gs_ref, k_ref, vt_ref)


def _kv_prep(ckv, kv_norm, w_uk, w_uvt, ind, gcos, gsin, *, bsz, seq, tm, vt_width):
    tiles = seq // tm
    sub = vt_width // tm
    ckv_blk = MLA_Q_LORA // MLA_KV_LORA
    kra_blk = (MLA_Q_LORA + MLA_KV_LORA) // LANE
    tab = pl.BlockSpec((tm, LANE), lambda i: (i % tiles, 0))
    return pl.pallas_call(
        _kv_prep_body,
        grid=(bsz * tiles,),
        in_specs=[pl.BlockSpec((tm, MLA_KV_LORA), lambda i: (i, ckv_blk)),
                  pl.BlockSpec((tm, LANE), lambda i: (i, kra_blk)),
                  pl.BlockSpec((tm, LANE), lambda i: (i, kra_blk + 1)),
                  _const_spec((1, MLA_KV_LORA)), _const_spec(w_uk.shape), _const_spec(w_uvt.shape),
                  _const_spec(ind.shape), tab, tab],
        out_specs=[pl.BlockSpec((1, MLA_HEADS, tm, LANE), lambda i: (i // tiles, 0, i % tiles, 0)),
                   pl.BlockSpec((1, MLA_HEADS, 1, VT_ROWS, tm),
                                lambda i: (i // tiles, 0, (i % tiles) // sub, 0, (i % tiles) % sub))],
        out_shape=[jax.ShapeDtypeStruct((bsz, MLA_HEADS, seq, LANE), BF16),
                   jax.ShapeDtypeStruct((bsz, MLA_HEADS, seq // vt_width, VT_ROWS, vt_width), BF16)],
        compiler_params=_cparams("parallel"),
        name="mla_kv_prep",
    )(ckv, ckv, ckv, kv_norm.reshape(1, -1), w_uk, w_uvt, ind, gcos, gsin)


def _attn_body(q_ref, kc_ref, vtc_ref, kl_ref, vtl_ref, o_ref, s_a, s_b, p_a, p_b, acc_scr, m_scr, mb_scr,
               *, n_lat, heads, unroll):
    qs = [q_ref[0, hh] for hh in range(heads)]

    def scores(kt, hh):
        return lax.dot_general(kt, qs[hh], (((1,), (1,)), ((), ())), preferred_element_type=F32)

    for hh in range(heads):
        st = scores(kc_ref[0, hh], hh)
        m = jnp.max(st, axis=0, keepdims=True)
        m_scr[hh] = m
        acc_scr[hh] = jnp.dot(vtc_ref[0, hh, 0], jnp.exp2(st - m).astype(BF16), preferred_element_type=F32)
        st = scores(kl_ref[0, hh, 0], hh)
        s_a[hh] = st
        mb_scr[hh] = jnp.max(st, axis=0, keepdims=True)
        p_b[hh] = jnp.zeros(p_b.shape[1:], p_b.dtype)

    def block_step(j, s_cur, s_nxt, p_wr, p_rd):
        jn = jnp.minimum(j + 1, n_lat - 1)
        jp = jnp.maximum(j - 1, 0)
        for hh in range(heads):
            st_n = scores(kl_ref[0, hh, jn], hh)
            s_nxt[hh] = st_n
            pv = jnp.dot(vtl_ref[0, hh, jp], p_rd[hh], preferred_element_type=F32)
            m = m_scr[hh]
            m_new = jnp.maximum(m, mb_scr[hh])
            p_wr[hh] = jnp.exp2(s_cur[hh] - m_new).astype(BF16)
            acc_scr[hh] = jnp.exp2(m - m_new) * (acc_scr[hh] + pv)
            m_scr[hh] = m_new
            mb_scr[hh] = jnp.max(st_n, axis=0, keepdims=True)

    def trip(t, carry):
        for u in range(0, unroll, 2):
            block_step(unroll * t + u, s_a, s_b, p_a, p_b)
            block_step(unroll * t + u + 1, s_b, s_a, p_b, p_a)
        return carry

    lax.fori_loop(0, n_lat // unroll, trip, 0)
    outs = []
    for hh in range(heads):
        acc = acc_scr[hh] + jnp.dot(vtl_ref[0, hh, n_lat - 1], p_b[hh], preferred_element_type=F32)
        outs.append(acc[:MLA_V] / acc[MLA_V:MLA_V + 1])
    o_ref[0] = jnp.transpose(jnp.concatenate(outs, axis=0)).astype(o_ref.dtype)


def _attention(q, k_ctx, vt_ctx, k_lat, vt_lat, *, tq):
    bsz, nh, seq, _ = q.shape
    heads = LANE // MLA_V
    n_c = k_ctx.shape[2]
    n_lat, _, kb = vt_lat.shape[2:]
    unroll = 4 if n_lat % 4 == 0 else 2
    assert n_lat % unroll == 0
    k_lat = k_lat.reshape(bsz, nh, n_lat, kb, LANE)
    return pl.pallas_call(
        functools.partial(_attn_body, n_lat=n_lat, heads=heads, unroll=unroll),
        grid=(bsz, nh // heads, seq // tq),
        in_specs=[pl.BlockSpec((1, heads, tq, LANE), lambda b, hp, i: (b, hp, i, 0)),
                  pl.BlockSpec((1, heads, n_c, LANE), lambda b, hp, i: (b, hp, 0, 0)),
                  pl.BlockSpec((1, heads, 1, VT_ROWS, n_c), lambda b, hp, i: (b, hp, 0, 0, 0)),
                  pl.BlockSpec((1, heads, n_lat, kb, LANE), lambda b, hp, i: (b, hp, 0, 0, 0)),
                  pl.BlockSpec((1, heads, n_lat, VT_ROWS, kb), lambda b, hp, i: (b, hp, 0, 0, 0))],
        out_specs=pl.BlockSpec((1, tq, LANE), lambda b, hp, i: (b, i, hp)),
        out_shape=jax.ShapeDtypeStruct((bsz, seq, nh * MLA_V), BF16),
        scratch_shapes=[pltpu.VMEM((heads, kb, tq), F32), pltpu.VMEM((heads, kb, tq), F32),
                        pltpu.VMEM((heads, kb, tq), BF16), pltpu.VMEM((heads, kb, tq), BF16),
                        pltpu.VMEM((heads, VT_ROWS, tq), F32), pltpu.VMEM((heads, 1, tq), F32),
                        pltpu.VMEM((heads, 1, tq), F32)],
        compiler_params=_cparams("parallel", "parallel", "arbitrary"),
        name="mla_attention",
    )(q, k_ctx, vt_ctx, k_lat, vt_lat)


def _attn_bounded_body(q_ref, kc_ref, vtc_ref, kl_ref, vtl_ref, shift_ref, o_ref, p_a, p_b, acc_scr,
                       *, n_lat, heads, unroll):
    shift = shift_ref[...].astype(BF16)
    qs = [q_ref[0, hh] + shift for hh in range(heads)]

    def probs(kt, hh):
        st = lax.dot_general(kt, qs[hh], (((1,), (1,)), ((), ())), preferred_element_type=F32)
        return jnp.exp2(st).astype(BF16)

    for hh in range(heads):
        acc_scr[hh] = jnp.dot(vtc_ref[0, hh, 0], probs(kc_ref[0, hh], hh), preferred_element_type=F32)
        p_a[hh] = probs(kl_ref[0, hh, 0], hh)

    def block_step(j, p_cur, p_nxt, produce):
        for hh in range(heads):
            if produce:
                p_nxt[hh] = probs(kl_ref[0, hh, j + 1], hh)
            acc_scr[hh] += jnp.dot(vtl_ref[0, hh, j], p_cur[hh], preferred_element_type=F32)

    def steps(j0, last):
        for u in range(0, unroll, 2):
            block_step(j0 + u, p_a, p_b, True)
            block_step(j0 + u + 1, p_b, p_a, not (last and u + 2 == unroll))

    def trip(t, carry):
        steps(unroll * t, False)
        return carry

    lax.fori_loop(0, n_lat // unroll - 1, trip, 0)
    steps(n_lat - unroll, True)
    outs = [acc_scr[hh][:MLA_V] / acc_scr[hh][MLA_V:MLA_V + 1] for hh in range(heads)]
    o_ref[0] = jnp.transpose(jnp.concatenate(outs, axis=0)).astype(o_ref.dtype)


def _attention_bounded(q, k_ctx, vt_ctx, k_lat, vt_lat, shift, *, tq):
    bsz, nh, seq, _ = q.shape
    heads = LANE // MLA_V
    n_c = k_ctx.shape[2]
    n_lat, _, kb = vt_lat.shape[2:]
    unroll = 4 if n_lat % 4 == 0 else 2
    assert n_lat % unroll == 0
    k_lat = k_lat.reshape(bsz, nh, n_lat, kb, LANE)
    return pl.pallas_call(
        functools.partial(_attn_bounded_body, n_lat=n_lat, heads=heads, unroll=unroll),
        grid=(bsz, nh // heads, seq // tq),
        in_specs=[pl.BlockSpec((1, heads, tq, LANE), lambda b, hp, i: (b, hp, i, 0)),
                  pl.BlockSpec((1, heads, n_c, LANE), lambda b, hp, i: (b, hp, 0, 0)),
                  pl.BlockSpec((1, heads, 1, VT_ROWS, n_c), lambda b, hp, i: (b, hp, 0, 0, 0)),
                  pl.BlockSpec((1, heads, n_lat, kb, LANE), lambda b, hp, i: (b, hp, 0, 0, 0)),
                  pl.BlockSpec((1, heads, n_lat, VT_ROWS, kb), lambda b, hp, i: (b, hp, 0, 0, 0)),
                  pl.BlockSpec((1, LANE), lambda b, hp, i: (0, 0))],
        out_specs=pl.BlockSpec((1, tq, LANE), lambda b, hp, i: (b, i, hp)),
        out_shape=jax.ShapeDtypeStruct((bsz, seq, nh * MLA_V), BF16),
        scratch_shapes=[pltpu.VMEM((heads, kb, tq), BF16), pltpu.VMEM((heads, kb, tq), BF16),
                        pltpu.VMEM((heads, VT_ROWS, tq), F32)],
        compiler_params=_cparams("parallel", "parallel", "arbitrary"),
        name="mla_attention_bounded",
    )(q, k_ctx, vt_ctx, k_lat, vt_lat, shift)


def _pad_heads(w, heads, dim):
    k = w.shape[0]
    w = w.reshape(k, heads, dim)
    return jnp.pad(w, ((0, 0), (0, 0), (0, LANE - dim))).reshape(k, heads * LANE)


def _swap_pairs(a):
    s = a.shape
    return a.reshape(*s[:-1], s[-1] // 2, 2)[..., ::-1].reshape(s)


def _rope_lanes(a, swapped):
    rope = a[..., MLA_NOPE:]
    lo = jnp.zeros_like(a[..., :MLA_NOPE]) if swapped else a[..., :MLA_NOPE]
    mid = _swap_pairs(rope) if swapped else rope
    return jnp.concatenate([lo, mid, jnp.zeros_like(rope)], axis=-1)


def _rope_tables(rows, gain):
    row = jnp.repeat(jnp.arange(rows, dtype=F32), GRID_W)
    col = jnp.tile(jnp.arange(GRID_W, dtype=F32), rows)
    half = MLA_ROPE // 2
    inv_freq = ROPE_BASE ** (-jnp.arange(0, half, 2, dtype=F32) / half)
    ang = jnp.concatenate([row[:, None] * inv_freq, col[:, None] * inv_freq], axis=-1)
    cos = jnp.repeat(jnp.cos(ang), 2, axis=-1)
    sin = jnp.repeat(jnp.sin(ang), 2, axis=-1) * jnp.tile(jnp.array([-1.0, 1.0], F32), MLA_ROPE // 2)
    n = rows * GRID_W
    cos_tab = jnp.concatenate([jnp.ones((n, MLA_NOPE), F32), cos, jnp.zeros((n, MLA_ROPE), F32)], axis=-1)
    sin_tab = jnp.concatenate([jnp.zeros((n, MLA_NOPE), F32), sin, jnp.zeros((n, MLA_ROPE), F32)], axis=-1)
    return cos_tab * _rope_lanes(gain, False), sin_tab * _rope_lanes(gain, True)


def _no_rope_tables(n, gain):
    return jnp.broadcast_to(_rope_lanes(gain, False), (n, LANE)), jnp.zeros((n, LANE), F32)


def _row_tile(seq, want):
    t = min(want, seq)
    assert seq % t == 0 and t % HALO == 0
    return t


def kernel(x, c, ctx, c_ctx, ada_w, ada_b, ab_w_in, ab_gate_bias, ab_conv_w, ab_conv_b, ab_ln_g, ab_ln_b,
           ab_head_gain, ab_w_out, mla_w_in, mla_q_norm, mla_kv_norm, mla_w_uq, mla_w_ukv, mla_q_gain,
           mla_k_gain, mla_w_out, ffn_w_in, ffn_conv_w, ffn_conv_b, ffn_w_out):
    bsz, seq, d = x.shape
    n_ctx = ctx.shape[1]
    assert bsz <= 7 and seq % GRID_W == 0
    tm_l = _row_tile(seq, 512)
    tm_c = _row_tile(n_ctx, 512)
    chunk_l = _row_tile(seq, 256)
    chunk_c = _row_tile(n_ctx, 256)

    c8 = jnp.zeros((8, d), F32).at[:bsz].set(c).at[bsz].set(c_ctx)
    mods = _ada_mods(c8, ada_w, ada_b)

    def mod_vecs(layer):
        parts = jnp.split(mods[layer], 6, axis=-1)
        lat = [p[:bsz, None, :] for p in parts]
        cx = [jnp.broadcast_to(p[bsz][None, None, :], (bsz, 1, d)) for p in parts]
        return lat, cx

    xl = x.reshape(bsz * seq, d)
    xc = ctx.reshape(bsz * n_ctx, d)

    def conv_ffn(xr, mod, layer, seq_len, tm, pre=None):
        shift, scale, gate = mod[3], mod[4], mod[5]
        return _conv_ffn(xr, 1.0 + scale, shift, gate, ffn_w_in[layer].astype(BF16), ffn_conv_w[layer],
                         ffn_conv_b[layer], ffn_w_out[layer].astype(BF16), seq=seq_len, tm=tm, pre=pre)

    lat, cx = mod_vecs(0)
    w_in = ab_w_in[0]
    wa, wg, wq, wk, wv, wo, wgt = jnp.split(
        w_in, [512, 1024, 1280, 1536, 2048, 2560], axis=1)
    w_main = jnp.concatenate([wa, wg, wv, wo, _pad_heads(wq, MLSTM_HEADS, MLSTM_QK)], axis=1).astype(BF16)
    u_blk, v_blk, o_blk, q_blk = 0, 1, 2, 3
    w_kt = jnp.transpose(_pad_heads(wk * (MLSTM_QK ** -0.5), MLSTM_HEADS, MLSTM_QK)).astype(BF16)
    wgi = jnp.concatenate([wgt[:, 0:4], wgt[:, 8:12]], axis=1)
    wgf = jnp.concatenate([wgt[:, 4:8], wgt[:, 12:16]], axis=1)
    pad8 = lambda a: jnp.pad(a, ((0, 0), (0, LANE - 8)))
    wgate = jnp.concatenate([pad8(wgi), pad8(wgf)], axis=1)
    wgate_hi = wgate.astype(BF16)
    wgate_lo = (wgate - wgate_hi.astype(F32)).astype(BF16)
    w_gate = jnp.stack([wgate_hi, wgate_lo])
    gb = ab_gate_bias[0]
    gbias = jnp.concatenate([pad8(jnp.concatenate([gb[0:4], gb[8:12]])[None, :]),
                             pad8(jnp.concatenate([gb[4:8], gb[12:16]])[None, :])], axis=1)
    w_out0 = ab_w_out[0].astype(BF16)

    def mixer0(xr, mod, seq_len, tm, chunk, s0, m0):
        y, kt, gates = _ab_in(xr, 1.0 + mod[1], mod[0], w_main, w_kt, w_gate, bsz=bsz, seq=seq_len, tm=tm)
        gc, gr = _gate_prep(gates, gbias, bsz=bsz, seq=seq_len, chunk=chunk)
        hf, hb, s1, m1 = _mlstm(y, kt, gc, gr, s0, m0, bsz=bsz, seq=seq_len, chunk=chunk,
                                q_blk=q_blk, v_blk=v_blk)
        out = _merge(y, hf.reshape(-1, hf.shape[-1]), hb.reshape(-1, hb.shape[-1]), ab_conv_w[0], ab_conv_b[0],
                     ab_ln_g[0], ab_ln_b[0], ab_head_gain[0], xr, mod[2], w_out0, seq=seq_len, tm=tm,
                     u_blk=u_blk, o_blk=o_blk)
        return out, s1, m1

    nch = 2 * MLSTM_HEADS
    s_zero = jnp.zeros((bsz, nch, LANE, 2 * LANE), F32)
    m_zero = jnp.zeros((bsz, nch, 8, LANE), F32)
    xc, s_ctx, m_ctx = mixer0(xc, cx, n_ctx, tm_c, chunk_c, s_zero, m_zero)
    xl, _, _ = mixer0(xl, lat, seq, tm_l, chunk_l, s_ctx, m_ctx)
    xl = conv_ffn(xl, lat, 0, seq, tm_l)
    xc = conv_ffn(xc, cx, 0, n_ctx, tm_c)

    lat, cx = mod_vecs(1)
    w_in = mla_w_in[0]
    w_kr = w_in[:, MLA_Q_LORA + MLA_KV_LORA:]
    zpad = lambda n: jnp.zeros((d, n), F32)
    w_in1 = jnp.concatenate([w_in[:, :MLA_Q_LORA + MLA_KV_LORA],
                             zpad(MLA_NOPE), w_kr, zpad(MLA_ROPE),
                             zpad(MLA_NOPE), _swap_pairs(w_kr), zpad(MLA_ROPE)], axis=1).astype(BF16)
    w_uq = mla_w_uq[0].reshape(MLA_Q_LORA, MLA_HEADS, MLA_QK)
    w_uq_a = _rope_lanes(w_uq, False).reshape(MLA_Q_LORA, MLA_HEADS * LANE).astype(BF16)
    w_uq_b = _rope_lanes(w_uq, True).reshape(MLA_Q_LORA, MLA_HEADS * LANE).astype(BF16)
    w_ukv = mla_w_ukv[0].reshape(MLA_KV_LORA, MLA_HEADS, MLA_NOPE + MLA_V)
    w_uk = _pad_heads(w_ukv[..., :MLA_NOPE].reshape(MLA_KV_LORA, -1), MLA_HEADS, MLA_NOPE).astype(BF16)
    w_uvt = jnp.transpose(w_ukv[..., MLA_NOPE:].reshape(MLA_KV_LORA, -1)).astype(BF16)
    ind = (jnp.arange(MLA_HEADS * LANE)[:, None] // LANE == jnp.arange(LANE)[None, :]).astype(BF16)
    q_tabs = _rope_tables(seq // GRID_W, mla_q_gain[0] * (MLA_QK ** -0.5 * math.log2(math.e)))
    k_tabs = _rope_tables(seq // GRID_W, mla_k_gain[0])
    kc_tabs = _no_rope_tables(n_ctx, mla_k_gain[0])

    tk = 256
    kv_blk = min(1024, seq // 2)
    assert seq % kv_blk == 0 and kv_blk % tk == 0 and n_ctx % tk == 0
    c_ctx_ = _mm(xc, w_in1, tm=tm_c, out_dtype=F32, group_tiles=n_ctx // tm_c,
                 pro=(1.0 + cx[1], cx[0]), name="mla_in")
    q, k_lat, vt_lat = _mla_prep(xl, 1.0 + lat[1], lat[0], w_in1, mla_q_norm[0], mla_kv_norm[0], w_uq_a, w_uq_b,
                                 w_uk, w_uvt, ind, q_tabs, k_tabs, bsz=bsz, seq=seq, tm=tk, vt_width=kv_blk)
    k_ctx, vt_ctx = _kv_prep(c_ctx_, mla_kv_norm[0], w_uk, w_uvt, ind, *kc_tabs, bsz=bsz, seq=n_ctx, tm=tk,
                             vt_width=n_ctx)
    bound = (1.02 * MLA_QK ** 0.5 * math.log2(math.e)) * jnp.max(jnp.abs(mla_q_gain[0])) * jnp.max(
        jnp.abs(mla_k_gain[0]))
    shift = jnp.zeros((1, LANE), F32).at[0, MLA_QK].set(-bound)
    att = lax.cond(
        2.0 * bound < MAX_EXP2_SPAN,
        lambda: _attention_bounded(q, k_ctx, vt_ctx, k_lat, vt_lat, shift, tq=min(2048, seq)),
        lambda: _attention(q, k_ctx, vt_ctx, k_lat, vt_lat, tq=min(512, seq)))
    xl = conv_ffn(xl, lat, 1, seq, tm_l, pre=(att.reshape(bsz * seq, -1), mla_w_out[0].astype(BF16), lat[2]))
    return xl.reshape(bsz, seq, d)
```
